```python
import math
import jax, jax.numpy as jnp
from jax import lax
import numpy as np


D_MODEL = 2048
BATCH = 4
SEQ = 2048
DEPTH = 4

NSA_HEADS = 8
NSA_KV_GROUPS = 2
NSA_HEAD_DIM = 128
CMP_LEN = 32
CMP_STRIDE = 16
SLC_BLOCK = 64
SLC_TOPK = 8
WINDOW = 512
WIN_QBLOCK = 128
SLC_QBLOCK = 64
SSD_INNER = D_MODEL
SSD_HEAD_DIM = 64
SSD_HEADS = SSD_INNER // SSD_HEAD_DIM
SSD_GROUPS = 4
SSD_STATE = 128
SSD_CONV = 4
SSD_CHUNK = 128
RET_HEADS = 4
RET_QK_DIM = 128
RET_V_DIM = 256
RET_CHUNK = 128
D_FF = -(-8 * D_MODEL // (3 * 256)) * 256

NSA_Q = NSA_HEADS * NSA_HEAD_DIM
NSA_KV = NSA_KV_GROUPS * NSA_HEAD_DIM
SSD_BC = SSD_GROUPS * SSD_STATE
SSD_CONV_DIM = SSD_INNER + 2 * SSD_BC
RET_QK = RET_HEADS * RET_QK_DIM
RET_V = RET_HEADS * RET_V_DIM
IN_SPLITS = (NSA_Q, 6 * NSA_KV, 3 * NSA_HEADS, SSD_INNER, SSD_CONV_DIM, SSD_HEADS, 2 * RET_QK, RET_V, RET_V, 3 * D_MODEL)
D_IN = sum(IN_SPLITS)
NEG_INF = -1e30
FORCE_SCORE = 1e9
EPS = 1e-6

kernel_name = 'hybrid_nsa_ssd_retention_block'


def rms_norm(x, w):
    xf = x.astype(jnp.float32)
    y = xf * lax.rsqrt(jnp.mean(xf * xf, axis=-1, keepdims=True) + EPS)
    return (y * w.astype(jnp.float32)).astype(x.dtype)


def alibi_slopes(n):
    return jnp.exp2(-8.0 * (jnp.arange(n, dtype=jnp.float32) + 1.0) / n)


def masked_softmax(s, mask):
    return jax.nn.softmax(jnp.where(mask, s, NEG_INF), axis=-1)


def nsa_mixer(q, kv, gate_logits, k_pe, k_w1, k_w2, v_pe, v_w1, v_w2):
    f32 = jnp.float32
    B, S, _ = q.shape
    G, Dh = NSA_KV_GROUPS, NSA_HEAD_DIM
    J = NSA_HEADS // G
    q = q.reshape(B, S, G, J, Dh) * (Dh ** -0.5)
    kc, vc, ks, vs, kw, vw = [t.reshape(B, S, G, Dh) for t in jnp.split(kv, 6, axis=-1)]
    slopes = alibi_slopes(NSA_HEADS).reshape(G, J)
    pos = jnp.arange(S)

    n_cmp = (S - CMP_LEN) // CMP_STRIDE + 1
    cmp_start = jnp.arange(n_cmp) * CMP_STRIDE
    cmp_end = cmp_start + CMP_LEN - 1
    blk_idx = cmp_start[:, None] + jnp.arange(CMP_LEN)[None, :]

    def compress(t, pe, w1, w2):
        blocks = t[:, blk_idx] + pe[None, None, :, None, :]
        blocks = blocks.transpose(0, 1, 3, 2, 4).reshape(B, n_cmp, G, CMP_LEN * Dh)
        return jax.nn.gelu(blocks @ w1) @ w2

    k_cmp = compress(kc, k_pe, k_w1, k_w2)
    v_cmp = compress(vc, v_pe, v_w1, v_w2)
    d_cmp = (pos[:, None] - cmp_end[None, :]).astype(f32)
    ok_cmp = d_cmp >= 0
    s_cmp = jnp.einsum('bsgjd,bngd->bgjsn', q, k_cmp).astype(f32) - slopes[:, :, None, None] * d_cmp
    p_cmp = masked_softmax(s_cmp, ok_cmp) * jnp.any(ok_cmp, axis=-1, keepdims=True)
    o_cmp = jnp.einsum('bgjsn,bngd->bsgjd', p_cmp.astype(v_cmp.dtype), v_cmp)

    n_slc = S // SLC_BLOCK
    top = min(SLC_TOPK, n_slc)
    slc_start = jnp.arange(n_slc) * SLC_BLOCK
    overlap = ((cmp_start[:, None] < slc_start[None, :] + SLC_BLOCK) & (cmp_end[:, None] >= slc_start[None, :])).astype(f32)
    imp = jnp.einsum('bgjsn,nk->bgsk', p_cmp, overlap)
    blk = jnp.arange(n_slc)[None, :]
    cur = (pos // SLC_BLOCK)[:, None]
    imp = jnp.where(blk > cur, NEG_INF, imp)
    imp = jnp.where((blk == cur) | (blk == 0), FORCE_SCORE, imp)
    sel = lax.top_k(imp, top)[1]

    ks_blk = ks.reshape(B, n_slc, SLC_BLOCK, G, Dh).transpose(0, 3, 1, 2, 4)
    vs_blk = vs.reshape(B, n_slc, SLC_BLOCK, G, Dh).transpose(0, 3, 1, 2, 4)
    nq = S // SLC_QBLOCK
    q_blocks = q.reshape(B, nq, SLC_QBLOCK, G, J, Dh).transpose(1, 0, 2, 3, 4, 5)
    sel_blocks = sel.reshape(B, G, nq, SLC_QBLOCK, top).transpose(2, 0, 1, 3, 4)
    bi = jnp.arange(B)[:, None, None, None]
    gi = jnp.arange(G)[None, :, None, None]

    def slc_block(args):
        qb, sb, c = args
        kb = ks_blk[bi, gi, sb]
        vb = vs_blk[bi, gi, sb]
        tq = c * SLC_QBLOCK + jnp.arange(SLC_QBLOCK)
        tk = sb[..., None] * SLC_BLOCK + jnp.arange(SLC_BLOCK)
        d = (tq[None, None, :, None, None] - tk).astype(f32)[:, :, None]
        sc = jnp.einsum('bqgjd,bgqnld->bgjqnl', qb, kb).astype(f32) - slopes[None, :, :, None, None, None] * d
        sc = jnp.where(d >= 0, sc, NEG_INF)
        Bq, Gq, Jq, Qq = sc.shape[:4]
        p = jax.nn.softmax(sc.reshape(Bq, Gq, Jq, Qq, -1), axis=-1).reshape(sc.shape)
        return jnp.einsum('bgjqnl,bgqnld->bqgjd', p.astype(vb.dtype), vb)

    o_slc = lax.map(slc_block, (q_blocks, sel_blocks, jnp.arange(nq)))
    o_slc = o_slc.transpose(1, 0, 2, 3, 4, 5).reshape(B, S, G, J, Dh)

    nw = S // WIN_QBLOCK
    span = WINDOW + WIN_QBLOCK
    kidx = jnp.arange(nw)[:, None] * WIN_QBLOCK + jnp.arange(span)[None, :] - WINDOW
    kidx_c = jnp.clip(kidx, 0, S - 1)
    kwb = kw[:, kidx_c]
    vwb = vw[:, kidx_c]
    qw = q.reshape(B, nw, WIN_QBLOCK, G, J, Dh)
    tq = pos.reshape(nw, WIN_QBLOCK)
    dw = tq[:, :, None] - kidx[:, None, :]
    ok_w = ((dw >= 0) & (dw < WINDOW) & (kidx[:, None, :] >= 0))[None, :, None, None]
    dwf = dw.astype(f32)[None, :, None, None]
    s_w = jnp.einsum('bcqgjd,bckgd->bcgjqk', qw, kwb).astype(f32) - slopes[None, None, :, :, None, None] * dwf
    p_w = masked_softmax(s_w, ok_w)
    o_win = jnp.einsum('bcgjqk,bckgd->bcqgjd', p_w.astype(vwb.dtype), vwb).reshape(B, S, G, J, Dh)

    gates = jax.nn.sigmoid(gate_logits.astype(f32)).reshape(B, S, G, J, 3).astype(o_cmp.dtype)
    out = gates[..., 0:1] * o_cmp + gates[..., 1:2] * o_slc + gates[..., 2:3] * o_win
    return out.reshape(B, S, NSA_Q)


def ssd_mixer(z, xbc, dt, conv_w, conv_b, dt_bias, a_log, d_skip, norm_w):
    f32 = jnp.float32
    B, S, _ = z.shape
    G, Hg, P, N, Lc = SSD_GROUPS, SSD_HEADS // SSD_GROUPS, SSD_HEAD_DIM, SSD_STATE, SSD_CHUNK
    nc = S // Lc
    xbc = lax.conv_general_dilated(xbc, conv_w[:, None, :].astype(xbc.dtype), window_strides=(1,), padding=[(SSD_CONV - 1, 0)], dimension_numbers=('NWC', 'WIO', 'NWC'), feature_group_count=SSD_CONV_DIM)
    xbc = jax.nn.silu((xbc + conv_b).astype(f32))
    xs, bm, cm = jnp.split(xbc, [SSD_INNER, SSD_INNER + SSD_BC], axis=-1)
    dt = jax.nn.softplus(dt.astype(f32) + dt_bias.astype(f32))
    a = (dt * -jnp.exp(a_log.astype(f32))).reshape(B, nc, Lc, G, Hg).transpose(0, 3, 4, 1, 2)
    x = xs.reshape(B, nc, Lc, G, Hg, P)
    xdt = x * dt.reshape(B, nc, Lc, G, Hg)[..., None]
    bmat = bm.reshape(B, nc, Lc, G, N)
    cmat = cm.reshape(B, nc, Lc, G, N)
    a_cum = jnp.cumsum(a, axis=-1)
    causal = jnp.tril(jnp.ones((Lc, Lc), dtype=bool))
    seg = a_cum[..., :, None] - a_cum[..., None, :]
    decay_in = jnp.exp(jnp.where(causal, seg, -jnp.inf))
    cb = jnp.einsum('bclgn,bcsgn->bgcls', cmat, bmat)
    y_diag = jnp.einsum('bghcls,bcsghp->bclghp', cb[:, :, None] * decay_in, xdt)
    decay_to_end = jnp.exp(a_cum[..., -1:] - a_cum)
    chunk_states = jnp.einsum('bclgn,bghcl,bclghp->bcghpn', bmat, decay_to_end, xdt)
    chunk_decay = jnp.exp(a_cum[..., -1])

    def step(h, inp):
        st, dec = inp
        return h * dec[..., None, None] + st, h

    h0 = jnp.zeros((B, G, Hg, P, N), f32)
    _, prev = lax.scan(step, h0, (chunk_states.transpose(1, 0, 2, 3, 4, 5), chunk_decay.transpose(3, 0, 1, 2)))
    prev = prev.transpose(1, 0, 2, 3, 4, 5)
    y_off = jnp.einsum('bclgn,bcghpn,bghcl->bclghp', cmat, prev, jnp.exp(a_cum))
    y = y_diag + y_off + x * d_skip.astype(f32).reshape(G, Hg, 1)
    y = y.reshape(B, S, SSD_INNER) * jax.nn.silu(z.astype(f32))
    y = y.reshape(B, S, G, SSD_INNER // G)
    y = y * lax.rsqrt(jnp.mean(y * y, axis=-1, keepdims=True) + EPS)
    return y.reshape(B, S, SSD_INNER) * norm_w.astype(f32)


def retention_mixer(q, k, v, g, norm_w):
    f32 = jnp.float32
    B, S, _ = q.shape
    H, Dk, Dv, Lc = RET_HEADS, RET_QK_DIM, RET_V_DIM, RET_CHUNK
    nc = S // Lc
    q = q.astype(f32).reshape(B, nc, Lc, H, Dk) * (Dk ** -0.5)
    k = k.astype(f32).reshape(B, nc, Lc, H, Dk)
    v = v.astype(f32).reshape(B, nc, Lc, H, Dv)
    log_g = jnp.log1p(-jnp.exp2(-5.0 - jnp.arange(H, dtype=f32)))
    idx = jnp.arange(Lc, dtype=f32)
    diff = idx[:, None] - idx[None, :]
    dmat = jnp.where(diff >= 0, jnp.exp(diff * log_g[:, None, None]), 0.0)
    scores = jnp.einsum('bclhd,bcshd->bchls', q, k) * dmat
    y_in = jnp.einsum('bchls,bcshv->bclhv', scores, v)
    k_dec = jnp.exp((Lc - 1.0 - idx)[None, :] * log_g[:, None])
    q_dec = jnp.exp((idx + 1.0)[None, :] * log_g[:, None])
    chunk_kv = jnp.einsum('bclhd,hl,bclhv->bchdv', k, k_dec, v)
    chunk_decay = jnp.exp(Lc * log_g)

    def step(r, kv_c):
        return r * chunk_decay[:, None, None] + kv_c, r

    r0 = jnp.zeros((B, H, Dk, Dv), f32)
    _, prev = lax.scan(step, r0, chunk_kv.transpose(1, 0, 2, 3, 4))
    prev = prev.transpose(1, 0, 2, 3, 4)
    y = y_in + jnp.einsum('bclhd,bchdv,hl->bclhv', q, prev, q_dec)
    mu = jnp.mean(y, axis=-1, keepdims=True)
    var = jnp.mean(jnp.square(y - mu), axis=-1, keepdims=True)
    y = (y - mu) * lax.rsqrt(var + EPS) * norm_w.astype(f32)
    return jax.nn.silu(g.astype(f32)) * y.reshape(B, S, RET_V)


def hybrid_mixer(u, w_in, k_pe, k_w1, k_w2, v_pe, v_w1, v_w2, conv_w, conv_b, dt_bias, a_log, d_skip, ssd_norm, ret_norm, p_nsa, p_ssd, p_ret, w_out):
    proj = u @ w_in
    offs = np.cumsum(IN_SPLITS)[:-1].tolist()
    nsa_q, nsa_kv, nsa_g, ssd_z, ssd_xbc, ssd_dt, ret_qk, ret_v, ret_g, merge_g = jnp.split(proj, offs, axis=-1)
    y_nsa = nsa_mixer(nsa_q, nsa_kv, nsa_g, k_pe, k_w1, k_w2, v_pe, v_w1, v_w2).astype(u.dtype)
    y_ssd = ssd_mixer(ssd_z, ssd_xbc, ssd_dt, conv_w, conv_b, dt_bias, a_log, d_skip, ssd_norm).astype(u.dtype)
    ret_q, ret_k = jnp.split(ret_qk, 2, axis=-1)
    y_ret = retention_mixer(ret_q, ret_k, ret_v, ret_g, ret_norm).astype(u.dtype)
    g_nsa, g_ssd, g_ret = jnp.split(jax.nn.sigmoid(merge_g), 3, axis=-1)
    merged = g_nsa * (y_nsa @ p_nsa) + g_ssd * (y_ssd @ p_ssd) + g_ret * (y_ret @ p_ret)
    return merged @ w_out


def setup_inputs(seed: int = 0):
    key = jax.random.key(seed)
    ks = jax.random.split(key, 32)
    f32 = jnp.float32
    L = DEPTH
    Dh = NSA_HEAD_DIM

    def nrm(k, shape, fan_in):
        return jax.random.normal(k, shape, f32) * (fan_in ** -0.5)

    def gain(k, shape):
        return 1.0 + 0.02 * jax.random.normal(k, shape, f32)

    dt0 = jnp.exp(jax.random.uniform(ks[12], (L, SSD_HEADS), f32, math.log(1e-3), math.log(1e-1)))
    return {
        'x': jax.random.normal(ks[0], (BATCH, SEQ, D_MODEL), f32),
        'norm_mix': gain(ks[1], (L, D_MODEL)),
        'w_in': nrm(ks[2], (L, D_MODEL, D_IN), D_MODEL),
        'cmp_k_pe': 0.02 * jax.random.normal(ks[3], (L, CMP_LEN, Dh), f32),
        'cmp_k_w1': nrm(ks[4], (L, CMP_LEN * Dh, Dh), CMP_LEN * Dh),
        'cmp_k_w2': nrm(ks[5], (L, Dh, Dh), Dh),
        'cmp_v_pe': 0.02 * jax.random.normal(ks[6], (L, CMP_LEN, Dh), f32),
        'cmp_v_w1': nrm(ks[7], (L, CMP_LEN * Dh, Dh), CMP_LEN * Dh),
        'cmp_v_w2': nrm(ks[8], (L, Dh, Dh), Dh),
        'conv_w': nrm(ks[9], (L, SSD_CONV, SSD_CONV_DIM), SSD_CONV),
        'conv_b': 0.02 * jax.random.normal(ks[10], (L, SSD_CONV_DIM), f32),
        'dt_bias': dt0 + jnp.log(-jnp.expm1(-dt0)),
        'a_log': jnp.log(jax.random.uniform(ks[13], (L, SSD_HEADS), f32, 1.0, 16.0)),
        'd_skip': 1.0 + 0.1 * jax.random.normal(ks[14], (L, SSD_HEADS), f32),
        'ssd_norm': gain(ks[15], (L, SSD_INNER)),
        'ret_norm': gain(ks[16], (L, RET_HEADS, RET_V_DIM)),
        'p_nsa': nrm(ks[17], (L, NSA_Q, D_MODEL), NSA_Q),
        'p_ssd': nrm(ks[18], (L, SSD_INNER, D_MODEL), SSD_INNER),
        'p_ret': nrm(ks[19], (L, RET_V, D_MODEL), RET_V),
        'w_out': nrm(ks[20], (L, D_MODEL, D_MODEL), D_MODEL),
        'norm_ffn': gain(ks[21], (L, D_MODEL)),
        'w_gate': nrm(ks[22], (L, D_MODEL, D_FF), D_MODEL),
        'w_up': nrm(ks[23], (L, D_MODEL, D_FF), D_MODEL),
        'w_down': nrm(ks[24], (L, D_FF, D_MODEL), D_FF),
        'norm_final': gain(ks[25], (D_MODEL,)),
    }


def reference(x, norm_mix, w_in, cmp_k_pe, cmp_k_w1, cmp_k_w2, cmp_v_pe, cmp_v_w1, cmp_v_w2, conv_w, conv_b, dt_bias, a_log, d_skip, ssd_norm, ret_norm, p_nsa, p_ssd, p_ret, w_out, norm_ffn, w_gate, w_up, w_down, norm_final):
    for l in range(DEPTH):
        u = rms_norm(x, norm_mix[l])
        mix = hybrid_mixer(u, w_in[l], cmp_k_pe[l], cmp_k_w1[l], cmp_k_w2[l], cmp_v_pe[l], cmp_v_w1[l], cmp_v_w2[l], conv_w[l], conv_b[l], dt_bias[l], a_log[l], d_skip[l], ssd_norm[l], ret_norm[l], p_nsa[l], p_ssd[l], p_ret[l], w_out[l])
        x = x + mix.astype(x.dtype)
        f = rms_norm(x, norm_ffn[l])
        x = x + ((jax.nn.silu(f @ w_gate[l]) * (f @ w_up[l])) @ w_down[l]).astype(x.dtype)
    return rms_norm(x, norm_final)
```

```python
import functools

import jax
import jax.numpy as jnp
import numpy as np
from jax import lax
from jax.experimental import pallas as pl
from jax.experimental.pallas import tpu as pltpu

F32 = jnp.float32
BF16 = jnp.bfloat16

D_MODEL = 2048
DEPTH = 4
NSA_HEADS = 8
NSA_GROUPS = 2
NSA_J = NSA_HEADS // NSA_GROUPS
DH = 128
CMP_LEN = 32
CMP_STRIDE = 16
SLC_BLOCK = 64
SLC_TOPK = 8
WINDOW = 512
SSD_INNER = D_MODEL
SSD_P = 64
SSD_HEADS = SSD_INNER // SSD_P
SSD_GROUPS = 4
SSD_HG = SSD_HEADS // SSD_GROUPS
SSD_N = 128
SSD_CONV = 4
CHUNK = 128
RET_HEADS = 4
RET_DK = 128
RET_DV = 256
D_FF = -(-8 * D_MODEL // (3 * 256)) * 256
NSA_Q = NSA_HEADS * DH
NSA_KV = NSA_GROUPS * DH
SSD_BC = SSD_GROUPS * SSD_N
SSD_CONV_DIM = SSD_INNER + 2 * SSD_BC
RET_QK = RET_HEADS * RET_DK
RET_V = RET_HEADS * RET_DV
IN_SPLITS = (NSA_Q, 6 * NSA_KV, 3 * NSA_HEADS, SSD_INNER, SSD_CONV_DIM, SSD_HEADS,
             2 * RET_QK, RET_V, RET_V, 3 * D_MODEL)
NEG_INF = -1e30
FORCE_SCORE = 1e9
BELOW_ALL = -3e38
EPS = 1e-6

LANE = 128

OFF_Q = 0
OFF_KV = OFF_Q + NSA_Q
OFF_Z = OFF_KV + 6 * NSA_KV
OFF_XBC = OFF_Z + SSD_INNER
OFF_RQK = OFF_XBC + SSD_CONV_DIM
OFF_RV = OFF_RQK + 2 * RET_QK
OFF_RG = OFF_RV + RET_V
OFF_MG = OFF_RG + RET_V
OFF_NG = OFF_MG + 3 * D_MODEL
OFF_DT = OFF_NG + NSA_GROUPS * LANE
N_USED = OFF_DT + SSD_GROUPS * LANE
PROJ_TN = 512
N_PACK = -(-N_USED // PROJ_TN) * PROJ_TN

ROW_TILE = 1024
VMEM_LIMIT = 56 * 1024 * 1024


def _dot(a, b):
    return jnp.dot(a, b, preferred_element_type=F32)


def _dot_nt(a, b):
    return lax.dot_general(a, b, (((1,), (1,)), ((), ())), preferred_element_type=F32)


def _split3(x):
    hi = x.astype(BF16)
    r1 = x - hi.astype(F32)
    mid = r1.astype(BF16)
    lo = (r1 - mid.astype(F32)).astype(BF16)
    return hi, mid, lo


def _dot_exact_rhs(x, m_bf16):
    hi, mid, lo = _split3(x)
    return _dot(hi, m_bf16) + _dot(mid, m_bf16) + _dot(lo, m_bf16)


def _dot_exact_lhs(m_bf16, x):
    hi, mid, lo = _split3(x)
    return _dot(m_bf16, hi) + _dot(m_bf16, mid) + _dot(m_bf16, lo)


def _params(sem):
    return pltpu.CompilerParams(dimension_semantics=sem, vmem_limit_bytes=VMEM_LIMIT)


def _rms_rows(x_ref, nw_ref, out_ref, rows):
    slab = 128

    def body(r, carry):
        sl = pl.ds(pl.multiple_of(r * slab, slab), slab)
        x = x_ref[sl, :]
        ms = jnp.mean(x * x, axis=-1, keepdims=True)
        out_ref[sl, :] = (x * lax.rsqrt(ms + EPS) * nw_ref[...]).astype(out_ref.dtype)
        return carry

    lax.fori_loop(0, rows // slab, body, 0)


def _inproj_body(x_ref, nw_ref, w_ref, o_ref, u_ref):
    @pl.when(pl.program_id(1) == 0)
    def _():
        _rms_rows(x_ref, nw_ref, u_ref, x_ref.shape[0])

    o_ref[...] = _dot(u_ref[...], w_ref[...]).astype(o_ref.dtype)


def _inproj(x, nw, w):
    t, d = x.shape
    n = w.shape[1]
    tm, tn = ROW_TILE, PROJ_TN
    return pl.pallas_call(
        _inproj_body,
        grid=(t // tm, n // tn),
        in_specs=[
            pl.BlockSpec((tm, d), lambda i, j: (i, 0)),
            pl.BlockSpec((1, d), lambda i, j: (0, 0)),
            pl.BlockSpec((d, tn), lambda i, j: (0, j)),
        ],
        out_specs=pl.BlockSpec((tm, tn), lambda i, j: (i, j)),
        out_shape=jax.ShapeDtypeStruct((t, n), BF16),
        scratch_shapes=[pltpu.VMEM((tm, d), BF16)],
        compiler_params=_params(("parallel", "arbitrary")),
        name="inproj",
    )(x, nw, w)


def _merge_body(yn_ref, ys_ref, yr_ref, gn_ref, gs_ref, gr_ref, pn_ref, ps_ref, pr_ref, o_ref):
    def gate(g_ref):
        return jax.nn.sigmoid(g_ref[...].astype(F32))

    acc = gate(gn_ref) * _dot(yn_ref[...], pn_ref[...])
    acc = acc + gate(gs_ref) * _dot(ys_ref[...], ps_ref[...])
    acc = acc + gate(gr_ref) * _dot(yr_ref[...], pr_ref[...])
    o_ref[...] = acc.astype(o_ref.dtype)


def _merge(y_nsa, y_ssd, y_ret, proj, p_nsa, p_ssd, p_ret):
    t = y_nsa.shape[0]
    tm, tn = ROW_TILE, 512
    g0 = OFF_MG // tn
    gstep = D_MODEL // tn
    return pl.pallas_call(
        _merge_body,
        grid=(t // tm, D_MODEL // tn),
        in_specs=[
            pl.BlockSpec((tm, NSA_Q), lambda i, j: (i, 0)),
            pl.BlockSpec((tm, SSD_INNER), lambda i, j: (i, 0)),
            pl.BlockSpec((tm, RET_V), lambda i, j: (i, 0)),
            pl.BlockSpec((tm, tn), lambda i, j: (i, g0 + j)),
            pl.BlockSpec((tm, tn), lambda i, j: (i, g0 + gstep + j)),
            pl.BlockSpec((tm, tn), lambda i, j: (i, g0 + 2 * gstep + j)),
            pl.BlockSpec((NSA_Q, tn), lambda i, j: (0, j)),
            pl.BlockSpec((SSD_INNER, tn), lambda i, j: (0, j)),
            pl.BlockSpec((RET_V, tn), lambda i, j: (0, j)),
        ],
        out_specs=pl.BlockSpec((tm, tn), lambda i, j: (i, j)),
        out_shape=jax.ShapeDtypeStruct((t, D_MODEL), BF16),
        compiler_params=_params(("parallel", "arbitrary")),
        name="merge",
    )(y_nsa, y_ssd, y_ret, proj, proj, proj, p_nsa, p_ssd, p_ret)


def _mm_res_body(a_ref, w_ref, r_ref, o_ref):
    o_ref[...] = r_ref[...] + _dot(a_ref[...], w_ref[...])


def _mm_res(a, w, res):
    t, k = a.shape
    n = w.shape[1]
    tm, tn = ROW_TILE, 512
    return pl.pallas_call(
        _mm_res_body,
        grid=(t // tm, n // tn),
        in_specs=[
            pl.BlockSpec((tm, k), lambda i, j: (i, 0)),
            pl.BlockSpec((k, tn), lambda i, j: (0, j)),
            pl.BlockSpec((tm, tn), lambda i, j: (i, j)),
        ],
        out_specs=pl.BlockSpec((tm, tn), lambda i, j: (i, j)),
        out_shape=jax.ShapeDtypeStruct((t, n), F32),
        compiler_params=_params(("parallel", "arbitrary")),
        name="mm_res",
    )(a, w, res)


def _ffn_up_body(x_ref, nw_ref, wg_ref, wu_ref, o_ref, f_ref):
    @pl.when(pl.program_id(1) == 0)
    def _():
        _rms_rows(x_ref, nw_ref, f_ref, x_ref.shape[0])

    f = f_ref[...]
    gate = _dot(f, wg_ref[...])
    up = _dot(f, wu_ref[...])
    o_ref[...] = (jax.nn.silu(gate) * up).astype(o_ref.dtype)


def _ffn_up(x, nw, wg, wu):
    t, d = x.shape
    n = wg.shape[1]
    tm, tn = ROW_TILE, 512
    return pl.pallas_call(
        _ffn_up_body,
        grid=(t // tm, n // tn),
        in_specs=[
            pl.BlockSpec((tm, d), lambda i, j: (i, 0)),
            pl.BlockSpec((1, d), lambda i, j: (0, 0)),
            pl.BlockSpec((d, tn), lambda i, j: (0, j)),
            pl.BlockSpec((d, tn), lambda i, j: (0, j)),
        ],
        out_specs=pl.BlockSpec((tm, tn), lambda i, j: (i, j)),
        out_shape=jax.ShapeDtypeStruct((t, n), BF16),
        scratch_shapes=[pltpu.VMEM((tm, d), BF16)],
        compiler_params=_params(("parallel", "arbitrary")),
        name="ffn_up",
    )(x, nw, wg, wu)


def _final_norm_body(x_ref, nw_ref, o_ref):
    _rms_rows(x_ref, nw_ref, o_ref, x_ref.shape[0])


def _final_norm(x, nw):
    t, d = x.shape
    tm = ROW_TILE
    return pl.pallas_call(
        _final_norm_body,
        grid=(t // tm,),
        in_specs=[pl.BlockSpec((tm, d), lambda i: (i, 0)), pl.BlockSpec((1, d), lambda i: (0, 0))],
        out_specs=pl.BlockSpec((tm, d), lambda i: (i, 0)),
        out_shape=jax.ShapeDtypeStruct((t, d), F32),
        compiler_params=_params(("parallel",)),
        name="final_norm",
    )(x, nw)


def _compress_body(x_ref, pe_ref, w1_ref, w2_ref, o_ref):
    half = (CMP_LEN // 2) * DH
    x = x_ref[...]
    a = _dot(x, w1_ref[:half, :])
    b = _dot(x, w1_ref[half:, :])
    bias = _dot(pe_ref[...], w1_ref[...])[0:1, :]
    nblk = x.shape[0]
    h = a + pltpu.roll(b, nblk - 1, 0) + bias
    h = jax.nn.gelu(h)
    o_ref[...] = _dot(h.astype(BF16), w2_ref[...]).astype(o_ref.dtype)


def _compress(xc, pe, w1, w2):
    b, _, g, nchunk, width = xc.shape
    return pl.pallas_call(
        _compress_body,
        grid=(b, 2, g),
        in_specs=[
            pl.BlockSpec((None, None, None, nchunk, width), lambda bi, m, gi: (bi, m, gi, 0, 0)),
            pl.BlockSpec((None, 8, CMP_LEN * DH), lambda bi, m, gi: (m, 0, 0)),
            pl.BlockSpec((None, CMP_LEN * DH, DH), lambda bi, m, gi: (m, 0, 0)),
            pl.BlockSpec((None, DH, DH), lambda bi, m, gi: (m, 0, 0)),
        ],
        out_specs=pl.BlockSpec((None, None, None, nchunk, DH), lambda bi, m, gi: (bi, m, gi, 0, 0)),
        out_shape=jax.ShapeDtypeStruct((b, 2, g, nchunk, DH), BF16),
        compiler_params=_params(("parallel", "arbitrary", "arbitrary")),
        name="nsa_compress",
    )(xc, pe, w1, w2)


NSA_TQ = 128
SLC_KCHUNK = 256
WIN_SPAN = WINDOW + NSA_TQ


def _nsa_body(q_ref, gate_ref, kc_ref, vc_ref, ks_ref, vs_ref, kw_ref, vw_ref, o_ref, *, seq):
    g = pl.program_id(1)
    i = pl.program_id(2)
    tq = NSA_TQ
    rows = NSA_J * tq
    n_slc = seq // SLC_BLOCK

    q = (q_ref[...].astype(F32) * (DH ** -0.5)).astype(BF16)
    q4 = jnp.concatenate([q[:, j * DH:(j + 1) * DH] for j in range(NSA_J)], axis=0)

    r = lax.broadcasted_iota(jnp.int32, (rows, 1), 0)
    head = r // tq
    tpos_i = i * tq + (r - head * tq)
    tpos = tpos_i.astype(F32)
    slope_g = jnp.where(g == 0, 0.5, 0.5 ** (NSA_J + 1)).astype(F32)
    slope = slope_g * jnp.where(head == 0, 1.0, jnp.where(head == 1, 0.5, jnp.where(head == 2, 0.25, 0.125)))

    ncmp = kc_ref.shape[0]
    n_idx = lax.broadcasted_iota(jnp.int32, (1, ncmp), 1)
    cmp_end = (n_idx * CMP_STRIDE + (CMP_LEN - 1)).astype(F32)
    d_cmp = tpos - cmp_end
    ok_cmp = d_cmp >= 0
    s = _dot_nt(q4, kc_ref[...]) - slope * d_cmp
    s = jnp.where(ok_cmp, s, NEG_INF)
    p = jnp.exp(s - jnp.max(s, axis=-1, keepdims=True))
    p = p / jnp.sum(p, axis=-1, keepdims=True)
    p = p * (tpos >= (CMP_LEN - 1)).astype(F32)
    o_cmp = _dot(p.astype(BF16), vc_ref[...])

    psum = p[0:tq]
    for j in range(1, NSA_J):
        psum = psum + p[j * tq:(j + 1) * tq]
    nn = lax.broadcasted_iota(jnp.int32, (ncmp, LANE), 0)
    kk = lax.broadcasted_iota(jnp.int32, (ncmp, LANE), 1)
    overlap = ((nn * CMP_STRIDE < (kk + 1) * SLC_BLOCK)
               & (nn * CMP_STRIDE + (CMP_LEN - 1) >= kk * SLC_BLOCK)
               & (kk < n_slc))
    imp = _dot_exact_rhs(psum, jnp.where(overlap, 1.0, 0.0).astype(BF16))
    blk = lax.broadcasted_iota(jnp.int32, (1, LANE), 1)
    blk_f = blk.astype(F32)
    tq_i = i * tq + lax.broadcasted_iota(jnp.int32, (tq, 1), 0)
    cur = tq_i // SLC_BLOCK
    val = jnp.where(blk > cur, NEG_INF, imp)
    val = jnp.where((blk == cur) | (blk == 0), FORCE_SCORE, val)
    val = jnp.where(blk >= n_slc, BELOW_ALL, val)
    sel = jnp.zeros((tq, LANE), F32)
    for _ in range(min(SLC_TOPK, n_slc)):
        mx = jnp.max(val, axis=-1, keepdims=True)
        idx = jnp.min(jnp.where(val == mx, blk_f, float(LANE)), axis=-1, keepdims=True)
        hit = blk_f == idx
        sel = jnp.where(hit, 1.0, sel)
        val = jnp.where(hit, BELOW_ALL, val)
    sel_bf = sel.astype(BF16)

    kc_cols = lax.broadcasted_iota(jnp.int32, (1, SLC_KCHUNK), 1)
    e_rows = lax.broadcasted_iota(jnp.int32, (LANE, SLC_KCHUNK), 0)
    e_cols = lax.broadcasted_iota(jnp.int32, (LANE, SLC_KCHUNK), 1) // SLC_BLOCK

    def slc_step(c, carry):
        m_run, l_run, acc = carry
        start = pl.multiple_of(c * SLC_KCHUNK, SLC_KCHUNK)
        k = ks_ref[pl.ds(start, SLC_KCHUNK), :]
        v = vs_ref[pl.ds(start, SLC_KCHUNK), :]
        expand = (e_rows == e_cols + c * (SLC_KCHUNK // SLC_BLOCK))
        chosen = _dot(sel_bf, jnp.where(expand, 1.0, 0.0).astype(BF16))
        chosen = jnp.concatenate([chosen] * NSA_J, axis=0)
        d = tpos - (start + kc_cols).astype(F32)
        sc = _dot_nt(q4, k) - slope * d
        sc = jnp.where((chosen > 0.5) & (d >= 0), sc, NEG_INF)
        m_new = jnp.maximum(m_run, jnp.max(sc, axis=-1, keepdims=True))
        alpha = jnp.exp(m_run - m_new)
        pr = jnp.exp(sc - m_new)
        l_new = alpha * l_run + jnp.sum(pr, axis=-1, keepdims=True)
        acc_new = alpha * acc + _dot(pr.astype(BF16), v)
        return m_new, l_new, acc_new

    n_chunks = (i * tq + tq - 1) // SLC_KCHUNK + 1
    init = (jnp.full((rows, 1), NEG_INF, F32), jnp.zeros((rows, 1), F32), jnp.zeros((rows, DH), F32))
    _, l_fin, acc_fin = lax.fori_loop(0, n_chunks, slc_step, init)
    o_slc = acc_fin / l_fin

    wstart = pl.multiple_of(jnp.maximum(i * tq - WINDOW, 0), tq)
    kw = kw_ref[pl.ds(wstart, WIN_SPAN), :]
    vw = vw_ref[pl.ds(wstart, WIN_SPAN), :]
    wkey = wstart + lax.broadcasted_iota(jnp.int32, (1, WIN_SPAN), 1)
    dw = tpos - wkey.astype(F32)
    sw = _dot_nt(q4, kw) - slope * dw
    sw = jnp.where((dw >= 0) & (dw < WINDOW), sw, NEG_INF)
    pw = jnp.exp(sw - jnp.max(sw, axis=-1, keepdims=True))
    pw = pw / jnp.sum(pw, axis=-1, keepdims=True)
    o_win = _dot(pw.astype(BF16), vw)

    gates = jax.nn.sigmoid(gate_ref[...].astype(F32))
    outs = []
    for j in range(NSA_J):
        sl = slice(j * tq, (j + 1) * tq)
        outs.append(gates[:, 3 * j:3 * j + 1] * o_cmp[sl]
                    + gates[:, 3 * j + 1:3 * j + 2] * o_slc[sl]
                    + gates[:, 3 * j + 2:3 * j + 3] * o_win[sl])
    o_ref[...] = jnp.concatenate(outs, axis=-1).astype(o_ref.dtype)


def _nsa(proj, cmp, batch, seq):
    t = proj.shape[0]
    nq = seq // NSA_TQ
    qw = NSA_J * DH
    kv0 = OFF_KV // DH

    def kv_spec(which):
        return pl.BlockSpec((seq, DH), lambda b, g, i: (b, kv0 + 2 * which + g))

    return pl.pallas_call(
        functools.partial(_nsa_body, seq=seq),
        grid=(batch, NSA_GROUPS, nq),
        in_specs=[
            pl.BlockSpec((NSA_TQ, qw), lambda b, g, i: (b * nq + i, g)),
            pl.BlockSpec((NSA_TQ, LANE), lambda b, g, i: (b * nq + i, OFF_NG // LANE + g)),
            pl.BlockSpec((None, None, None, seq // CMP_STRIDE, DH), lambda b, g, i: (b, 0, g, 0, 0)),
            pl.BlockSpec((None, None, None, seq // CMP_STRIDE, DH), lambda b, g, i: (b, 1, g, 0, 0)),
            kv_spec(2), kv_spec(3), kv_spec(4), kv_spec(5),
        ],
        out_specs=pl.BlockSpec((NSA_TQ, qw), lambda b, g, i: (b * nq + i, g)),
        out_shape=jax.ShapeDtypeStruct((t, NSA_Q), BF16),
        compiler_params=_params(("parallel", "arbitrary", "arbitrary")),
        name="nsa_attn",
    )(proj, proj, cmp, cmp, proj, proj, proj, proj)


def _causal_conv(raw_ref, prev_ref, w_ref, b_ref, first):
    cur = raw_ref[...].astype(F32)

    @pl.when(first)
    def _():
        prev_ref[...] = jnp.zeros_like(prev_ref)

    prev = prev_ref[...]
    rows = cur.shape[0]
    row = lax.broadcasted_iota(jnp.int32, cur.shape, 0)
    out = b_ref[...] + w_ref[SSD_CONV - 1:SSD_CONV, :] * cur
    for back in range(1, SSD_CONV):
        shifted = jnp.where(row < back, pltpu.roll(prev, back, 0), pltpu.roll(cur, back, 0))
        out = out + w_ref[SSD_CONV - 1 - back:SSD_CONV - back, :] * shifted
    prev_ref[...] = cur
    del rows
    return jax.nn.silu(out)


def _ssd_body(x_ref, b_ref, c_ref, z_ref, dt_ref, cwx_ref, cwb_ref, cwc_ref, cbx_ref, cbb_ref, cbc_ref,
              dtb_ref, alog_ref, dskip_ref, nw_ref, o_ref, state_ref, px_ref, pb_ref, pc_ref):
    first = pl.program_id(2) == 0
    lc = CHUNK
    width = SSD_HG * SSD_P

    @pl.when(first)
    def _():
        state_ref[...] = jnp.zeros_like(state_ref)

    xs = _causal_conv(x_ref, px_ref, cwx_ref, cbx_ref, first)
    bm = _causal_conv(b_ref, pb_ref, cwb_ref, cbb_ref, first)
    cm = _causal_conv(c_ref, pc_ref, cwc_ref, cbc_ref, first)

    dt = jax.nn.softplus(dt_ref[...].astype(F32) + dtb_ref[...])
    a = dt * -jnp.exp(alog_ref[...])
    li = lax.broadcasted_iota(jnp.int32, (lc, lc), 0)
    si = lax.broadcasted_iota(jnp.int32, (lc, lc), 1)
    causal = li >= si
    tril = jnp.where(causal, 1.0, 0.0).astype(BF16)
    a_cum = _dot_exact_lhs(tril, a)
    a_cum_t = a_cum.T

    eh = lax.broadcasted_iota(jnp.int32, (LANE, width), 0)
    ec = lax.broadcasted_iota(jnp.int32, (LANE, width), 1) // SSD_P
    spread = jnp.where(eh == ec, 1.0, 0.0).astype(BF16)
    dt_x = _dot_exact_rhs(dt, spread)
    acum_x = _dot_exact_rhs(a_cum, spread)
    last_x = acum_x[lc - 1:lc, :]

    xdt = xs * dt_x
    xdt_bf = xdt.astype(BF16)
    cm_bf = cm.astype(BF16)
    cb = _dot_nt(cm_bf, bm.astype(BF16))

    y_heads = []
    for h in range(SSD_HG):
        col = acum_x[:, h * SSD_P:h * SSD_P + 1]
        rowv = a_cum_t[h:h + 1, :]
        decay = jnp.exp(jnp.where(causal, col - rowv, -jnp.inf))
        y_heads.append(_dot((cb * decay).astype(BF16), xdt_bf[:, h * SSD_P:(h + 1) * SSD_P]))
    y_diag = jnp.concatenate(y_heads, axis=-1)

    state = state_ref[...]
    y_off = _dot(cm_bf, state.astype(BF16)) * jnp.exp(acum_x)
    to_end = jnp.exp(last_x - acum_x)
    chunk_state = _dot(bm.T.astype(BF16), (xdt * to_end).astype(BF16))
    state_ref[...] = state * jnp.exp(last_x) + chunk_state

    y = y_diag + y_off + xs * dskip_ref[...]
    y = y * jax.nn.silu(z_ref[...].astype(F32))
    y = y * lax.rsqrt(jnp.mean(y * y, axis=-1, keepdims=True) + EPS)
    o_ref[...] = (y * nw_ref[...]).astype(o_ref.dtype)


def _ssd(proj, cw_x, cw_b, cw_c, cb_x, cb_b, cb_c, dtb, alog, dskip, nw, batch, seq):
    t = proj.shape[0]
    nc = seq // CHUNK
    width = SSD_HG * SSD_P
    x0 = OFF_XBC // width
    b0 = (OFF_XBC + SSD_INNER) // SSD_N
    c0 = b0 + SSD_GROUPS
    z0 = OFF_Z // width
    d0 = OFF_DT // LANE
    row = lambda b, g, c: b * nc + c
    return pl.pallas_call(
        _ssd_body,
        grid=(batch, SSD_GROUPS, nc),
        in_specs=[
            pl.BlockSpec((CHUNK, width), lambda b, g, c: (row(b, g, c), x0 + g)),
            pl.BlockSpec((CHUNK, SSD_N), lambda b, g, c: (row(b, g, c), b0 + g)),
            pl.BlockSpec((CHUNK, SSD_N), lambda b, g, c: (row(b, g, c), c0 + g)),
            pl.BlockSpec((CHUNK, width), lambda b, g, c: (row(b, g, c), z0 + g)),
            pl.BlockSpec((CHUNK, LANE), lambda b, g, c: (row(b, g, c), d0 + g)),
            pl.BlockSpec((SSD_CONV, width), lambda b, g, c: (0, g)),
            pl.BlockSpec((SSD_CONV, SSD_N), lambda b, g, c: (0, g)),
            pl.BlockSpec((SSD_CONV, SSD_N), lambda b, g, c: (0, g)),
            pl.BlockSpec((1, width), lambda b, g, c: (0, g)),
            pl.BlockSpec((1, SSD_N), lambda b, g, c: (0, g)),
            pl.BlockSpec((1, SSD_N), lambda b, g, c: (0, g)),
            pl.BlockSpec((None, 1, LANE), lambda b, g, c: (g, 0, 0)),
            pl.BlockSpec((None, 1, LANE), lambda b, g, c: (g, 0, 0)),
            pl.BlockSpec((1, width), lambda b, g, c: (0, g)),
            pl.BlockSpec((1, width), lambda b, g, c: (0, g)),
        ],
        out_specs=pl.BlockSpec((CHUNK, width), lambda b, g, c: (row(b, g, c), g)),
        out_shape=jax.ShapeDtypeStruct((t, SSD_INNER), BF16),
        scratch_shapes=[
            pltpu.VMEM((SSD_N, width), F32),
            pltpu.VMEM((CHUNK, width), F32),
            pltpu.VMEM((CHUNK, SSD_N), F32),
            pltpu.VMEM((CHUNK, SSD_N), F32),
        ],
        compiler_params=_params(("parallel", "arbitrary", "arbitrary")),
        name="ssd",
    )(proj, proj, proj, proj, proj, cw_x, cw_b, cw_c, cb_x, cb_b, cb_c, dtb, alog, dskip, nw)


def _ret_body(q_ref, k_ref, v_ref, g_ref, nw_ref, o_ref, state_ref):
    h = pl.program_id(1)
    lc = CHUNK

    @pl.when(pl.program_id(2) == 0)
    def _():
        state_ref[...] = jnp.zeros_like(state_ref)

    hf = jnp.full((1, 1), h, jnp.int32).astype(F32)
    log_g = jnp.log(1.0 - jnp.exp2(-5.0 - hf))
    li = lax.broadcasted_iota(jnp.int32, (lc, lc), 0)
    si = lax.broadcasted_iota(jnp.int32, (lc, lc), 1)
    diff = (li - si).astype(F32)
    dmat = jnp.where(diff >= 0, jnp.exp(diff * log_g), 0.0)
    idx = lax.broadcasted_iota(jnp.int32, (lc, 1), 0).astype(F32)
    k_dec = jnp.exp((lc - 1.0 - idx) * log_g)
    q_dec = jnp.exp((idx + 1.0) * log_g)

    q = q_ref[...].astype(F32) * (RET_DK ** -0.5)
    k = k_ref[...].astype(F32)
    v = v_ref[...]
    scores = _dot_nt(q.astype(BF16), k.astype(BF16)) * dmat
    state = state_ref[...]
    y = _dot(scores.astype(BF16), v) + _dot((q * q_dec).astype(BF16), state.astype(BF16))
    chunk_kv = _dot((k * k_dec).T.astype(BF16), v)
    state_ref[...] = state * jnp.exp(lc * log_g) + chunk_kv

    mu = jnp.mean(y, axis=-1, keepdims=True)
    yc = y - mu
    var = jnp.mean(yc * yc, axis=-1, keepdims=True)
    y = yc * lax.rsqrt(var + EPS) * nw_ref[...]
    o_ref[...] = (jax.nn.silu(g_ref[...].astype(F32)) * y).astype(o_ref.dtype)


def _retention(proj, nw, batch, seq):
    t = proj.shape[0]
    nc = seq // CHUNK
    q0 = OFF_RQK // RET_DK
    k0 = q0 + RET_HEADS
    v0 = OFF_RV // RET_DV
    g0 = OFF_RG // RET_DV
    return pl.pallas_call(
        _ret_body,
        grid=(batch, RET_HEADS, nc),
        in_specs=[
            pl.BlockSpec((CHUNK, RET_DK), lambda b, h, c: (b * nc + c, q0 + h)),
            pl.BlockSpec((CHUNK, RET_DK), lambda b, h, c: (b * nc + c, k0 + h)),
            pl.BlockSpec((CHUNK, RET_DV), lambda b, h, c: (b * nc + c, v0 + h)),
            pl.BlockSpec((CHUNK, RET_DV), lambda b, h, c: (b * nc + c, g0 + h)),
            pl.BlockSpec((None, 1, RET_DV), lambda b, h, c: (h, 0, 0)),
        ],
        out_specs=pl.BlockSpec((CHUNK, RET_DV), lambda b, h, c: (b * nc + c, h)),
        out_shape=jax.ShapeDtypeStruct((t, RET_V), BF16),
        scratch_shapes=[pltpu.VMEM((RET_DK, RET_DV), F32)],
        compiler_params=_params(("parallel", "arbitrary", "arbitrary")),
        name="retention",
    )(proj, proj, proj, proj, nw)


def _pack_w_in(w_in):
    offs = np.cumsum(IN_SPLITS)[:-1].tolist()
    nsa_q, nsa_kv, nsa_g, ssd_z, ssd_xbc, ssd_dt, ret_qk, ret_v, ret_g, merge_g = jnp.split(w_in, offs, axis=-1)
    lead = w_in.shape[:-1]

    def per_group(seg, groups):
        per = seg.shape[-1] // groups
        seg = seg.reshape(*lead, groups, per)
        seg = jnp.pad(seg, [(0, 0)] * (seg.ndim - 1) + [(0, LANE - per)])
        return seg.reshape(*lead, groups * LANE)

    parts = [nsa_q, nsa_kv, ssd_z, ssd_xbc, ret_qk, ret_v, ret_g, merge_g,
             per_group(nsa_g, NSA_GROUPS), per_group(ssd_dt, SSD_GROUPS),
             jnp.zeros((*lead, N_PACK - N_USED), w_in.dtype)]
    return jnp.concatenate(parts, axis=-1).astype(BF16)


def _per_group_lanes(v, groups):
    depth, heads = v.shape
    per = heads // groups
    v = v.reshape(depth, groups, 1, per)
    return jnp.pad(v, [(0, 0), (0, 0), (0, 0), (0, LANE - per)])


def _layer(x, lp, batch, seq):
    proj = _inproj(x, lp["norm_mix"], lp["w_in"])

    kv = proj[:, OFF_KV:OFF_KV + 2 * NSA_KV].reshape(batch, seq // CMP_STRIDE, CMP_STRIDE, 2, NSA_GROUPS, DH)
    xc = kv.transpose(0, 3, 4, 1, 2, 5).reshape(batch, 2, NSA_GROUPS, seq // CMP_STRIDE, CMP_STRIDE * DH)
    cmp = _compress(xc, lp["cmp_pe"], lp["cmp_w1"], lp["cmp_w2"])

    y_nsa = _nsa(proj, cmp, batch, seq)
    y_ssd = _ssd(proj, lp["cw_x"], lp["cw_b"], lp["cw_c"], lp["cb_x"], lp["cb_b"], lp["cb_c"],
                 lp["dt_bias"], lp["a_log"], lp["d_skip"], lp["ssd_norm"], batch, seq)
    y_ret = _retention(proj, lp["ret_norm"], batch, seq)

    merged = _merge(y_nsa, y_ssd, y_ret, proj, lp["p_nsa"], lp["p_ssd"], lp["p_ret"])
    x = _mm_res(merged, lp["w_out"], x)
    hidden = _ffn_up(x, lp["norm_ffn"], lp["w_gate"], lp["w_up"])
    return _mm_res(hidden, lp["w_down"], x)


def _prepare_params(norm_mix, w_in, cmp_k_pe, cmp_k_w1, cmp_k_w2, cmp_v_pe, cmp_v_w1, cmp_v_w2, conv_w, conv_b,
                    dt_bias, a_log, d_skip, ssd_norm, ret_norm, p_nsa, p_ssd, p_ret, w_out, norm_ffn, w_gate,
                    w_up, w_down):
    depth = w_in.shape[0]
    pe = jnp.stack([cmp_k_pe, cmp_v_pe], axis=1).reshape(depth, 2, 1, CMP_LEN * DH)
    pe = jnp.pad(pe, [(0, 0), (0, 0), (0, 7), (0, 0)]).astype(BF16)
    stacked = {
        "norm_mix": norm_mix[:, None, :],
        "w_in": _pack_w_in(w_in),
        "cmp_pe": pe,
        "cmp_w1": jnp.stack([cmp_k_w1, cmp_v_w1], axis=1).astype(BF16),
        "cmp_w2": jnp.stack([cmp_k_w2, cmp_v_w2], axis=1).astype(BF16),
        "cw_x": conv_w[:, :, :SSD_INNER],
        "cw_b": conv_w[:, :, SSD_INNER:SSD_INNER + SSD_BC],
        "cw_c": conv_w[:, :, SSD_INNER + SSD_BC:],
        "cb_x": conv_b[:, None, :SSD_INNER],
        "cb_b": conv_b[:, None, SSD_INNER:SSD_INNER + SSD_BC],
        "cb_c": conv_b[:, None, SSD_INNER + SSD_BC:],
        "dt_bias": _per_group_lanes(dt_bias, SSD_GROUPS),
        "a_log": _per_group_lanes(a_log, SSD_GROUPS),
        "d_skip": jnp.repeat(d_skip, SSD_P, axis=-1)[:, None, :],
        "ssd_norm": ssd_norm[:, None, :],
        "ret_norm": ret_norm[:, :, None, :],
        "p_nsa": p_nsa.astype(BF16),
        "p_ssd": p_ssd.astype(BF16),
        "p_ret": p_ret.astype(BF16),
        "w_out": w_out.astype(BF16),
        "norm_ffn": norm_ffn[:, None, :],
        "w_gate": w_gate.astype(BF16),
        "w_up": w_up.astype(BF16),
        "w_down": w_down.astype(BF16),
    }
    return [{k: v[l] for k, v in stacked.items()} for l in range(depth)]


def kernel(x, norm_mix, w_in, cmp_k_pe, cmp_k_w1, cmp_k_w2, cmp_v_pe, cmp_v_w1, cmp_v_w2, conv_w, conv_b, dt_bias, a_log, d_skip, ssd_norm, ret_norm, p_nsa, p_ssd, p_ret, w_out, norm_ffn, w_gate, w_up, w_down, norm_final):
    batch, seq, d = x.shape
    layers = _prepare_params(norm_mix, w_in, cmp_k_pe, cmp_k_w1, cmp_k_w2, cmp_v_pe, cmp_v_w1, cmp_v_w2,
                             conv_w, conv_b, dt_bias, a_log, d_skip, ssd_norm, ret_norm, p_nsa, p_ssd, p_ret,
                             w_out, norm_ffn, w_gate, w_up, w_down)
    h = x.reshape(batch * seq, d)
    for lp in layers:
        h = _layer(h, lp, batch, seq)
    return _final_norm(h, norm_final[None, :]).reshape(batch, seq, d)
```

```python
import functools

import jax
import jax.numpy as jnp
import numpy as np
from jax import lax
from jax.experimental import pallas as pl
from jax.experimental.pallas import tpu as pltpu

F32 = jnp.float32
BF16 = jnp.bfloat16

D_MODEL = 2048
NSA_HEADS = 8
NSA_GROUPS = 2
NSA_J = NSA_HEADS // NSA_GROUPS
DH = 128
CMP_LEN = 32
CMP_STRIDE = 16
SLC_BLOCK = 64
SLC_TOPK = 8
WINDOW = 512
SSD_INNER = D_MODEL
SSD_P = 64
SSD_HEADS = SSD_INNER // SSD_P
SSD_GROUPS = 4
SSD_HG = SSD_HEADS // SSD_GROUPS
SSD_N = 128
SSD_CONV = 4
CHUNK = 128
RET_HEADS = 4
RET_DK = 128
RET_DV = 256
D_FF = -(-8 * D_MODEL // (3 * 256)) * 256
NSA_Q = NSA_HEADS * DH
NSA_KV = NSA_GROUPS * DH
SSD_BC = SSD_GROUPS * SSD_N
SSD_CONV_DIM = SSD_INNER + 2 * SSD_BC
RET_QK = RET_HEADS * RET_DK
RET_V = RET_HEADS * RET_DV
IN_SPLITS = (NSA_Q, 6 * NSA_KV, 3 * NSA_HEADS, SSD_INNER, SSD_CONV_DIM, SSD_HEADS,
             2 * RET_QK, RET_V, RET_V, 3 * D_MODEL)
D_IN = sum(IN_SPLITS)
NEG_INF = -1e30
FORCE_SCORE = 1e9
BELOW_ALL = -3e38
EPS = 1e-6

LANE = 128

OFF_Q = 0
OFF_KV = OFF_Q + NSA_Q
OFF_Z = OFF_KV + 6 * NSA_KV
OFF_XBC = OFF_Z + SSD_INNER
OFF_RQK = OFF_XBC + SSD_CONV_DIM
OFF_RV = OFF_RQK + 2 * RET_QK
OFF_RG = OFF_RV + RET_V
OFF_MG = OFF_RG + RET_V
OFF_NG = OFF_MG + 3 * D_MODEL
OFF_DT = OFF_NG + NSA_GROUPS * LANE
N_USED = OFF_DT + SSD_GROUPS * LANE
PROJ_TN = 512
N_PACK = -(-N_USED // PROJ_TN) * PROJ_TN
N_MAIN_BLOCKS = OFF_NG // PROJ_TN
N_SMALL = N_PACK - OFF_NG

SHIFT_B = IN_SPLITS[2]
SHIFT_C = IN_SPLITS[2] + IN_SPLITS[5]
BLK_B = OFF_Z // PROJ_TN
BLK_C = OFF_RQK // PROJ_TN

ROW_TILE = 2048
VMEM_LIMIT = 56 * 1024 * 1024


def _dot(a, b):
    return jnp.dot(a, b, preferred_element_type=F32)


def _dot_nt(a, b):
    return lax.dot_general(a, b, (((1,), (1,)), ((), ())), preferred_element_type=F32)


def _split3(x):
    hi = x.astype(BF16)
    r1 = x - hi.astype(F32)
    mid = r1.astype(BF16)
    lo = (r1 - mid.astype(F32)).astype(BF16)
    return hi, mid, lo


def _dot_exact_rhs(x, m_bf16):
    hi, mid, lo = _split3(x)
    return _dot(hi, m_bf16) + _dot(mid, m_bf16) + _dot(lo, m_bf16)


def _dot_exact_lhs(m_bf16, x):
    hi, mid, lo = _split3(x)
    return _dot(m_bf16, hi) + _dot(m_bf16, mid) + _dot(m_bf16, lo)


def _params(sem):
    return pltpu.CompilerParams(dimension_semantics=sem, vmem_limit_bytes=VMEM_LIMIT)


def _rms_body(x_ref, nw_ref, o_ref):
    slab = 128

    def body(r, carry):
        sl = pl.ds(pl.multiple_of(r * slab, slab), slab)
        x = x_ref[sl, :]
        ms = jnp.mean(x * x, axis=-1, keepdims=True)
        o_ref[sl, :] = (x * lax.rsqrt(ms + EPS) * nw_ref[...]).astype(o_ref.dtype)
        return carry

    lax.fori_loop(0, x_ref.shape[0] // slab, body, 0)


def _rms(x, nw, layer, out_dtype):
    t, d = x.shape
    tm = 1024
    return pl.pallas_call(
        _rms_body,
        grid=(t // tm,),
        in_specs=[pl.BlockSpec((tm, d), lambda i: (i, 0)),
                  pl.BlockSpec((None, 1, d), lambda i: (layer, 0, 0))],
        out_specs=pl.BlockSpec((tm, d), lambda i: (i, 0)),
        out_shape=jax.ShapeDtypeStruct((t, d), out_dtype),
        compiler_params=_params(("parallel",)),
        name="rmsnorm",
    )(x, nw)


def _inproj_body(u_ref, wa_ref, wb_ref, ws_ref, o_ref, w_ref):
    j = pl.program_id(1)
    slab = 128
    nslab = wa_ref.shape[0] // slab
    tn = wa_ref.shape[1]

    def shifted(shift):
        def body(r, carry):
            sl = pl.ds(pl.multiple_of(r * slab, slab), slab)
            w = jnp.concatenate([wa_ref[sl, :], wb_ref[sl, :]], axis=1)
            w_ref[sl, :] = pltpu.roll(w, tn + LANE - shift, 1)[:, :tn].astype(BF16)
            return carry

        lax.fori_loop(0, nslab, body, 0)

    @pl.when(j < BLK_B)
    def _():
        w_ref[...] = wa_ref[...].astype(BF16)

    @pl.when((j >= BLK_B) & (j < BLK_C))
    def _():
        shifted(SHIFT_B)

    @pl.when((j >= BLK_C) & (j < N_MAIN_BLOCKS))
    def _():
        shifted(SHIFT_C)

    @pl.when(j >= N_MAIN_BLOCKS)
    def _():
        w_ref[...] = ws_ref[...]

    o_ref[...] = _dot(u_ref[...], w_ref[...]).astype(o_ref.dtype)


def _inproj(u, w_in, w_small, layer):
    t, d = u.shape
    tm, tn = ROW_TILE, PROJ_TN
    last_main = N_MAIN_BLOCKS - 1
    per = tn // LANE
    return pl.pallas_call(
        _inproj_body,
        grid=(t // tm, N_PACK // tn),
        in_specs=[
            pl.BlockSpec((tm, d), lambda i, j: (i, 0)),
            pl.BlockSpec((None, d, tn), lambda i, j: (layer, 0, jnp.minimum(j, last_main))),
            pl.BlockSpec((None, d, LANE), lambda i, j: (layer, 0, per * (jnp.minimum(j, last_main) + 1))),
            pl.BlockSpec((None, d, tn), lambda i, j: (layer, 0, jnp.maximum(j - N_MAIN_BLOCKS, 0))),
        ],
        out_specs=pl.BlockSpec((tm, tn), lambda i, j: (i, j)),
        out_shape=jax.ShapeDtypeStruct((t, N_PACK), BF16),
        scratch_shapes=[pltpu.VMEM((d, tn), BF16)],
        compiler_params=_params(("parallel", "arbitrary")),
        name="inproj",
    )(u, w_in, w_in, w_small)


def _merge_body(yn_ref, ys_ref, yr_ref, gn_ref, gs_ref, gr_ref, pn_ref, ps_ref, pr_ref, o_ref):
    def gate(g_ref):
        return jax.nn.sigmoid(g_ref[...].astype(F32))

    acc = gate(gn_ref) * _dot(yn_ref[...], pn_ref[...].astype(BF16))
    acc = acc + gate(gs_ref) * _dot(ys_ref[...], ps_ref[...].astype(BF16))
    acc = acc + gate(gr_ref) * _dot(yr_ref[...], pr_ref[...].astype(BF16))
    o_ref[...] = acc.astype(o_ref.dtype)


def _merge(y_nsa, y_ssd, y_ret, proj, p_nsa, p_ssd, p_ret, layer):
    t = y_nsa.shape[0]
    tm, tn = 1024, 512
    g0 = OFF_MG // tn
    gstep = D_MODEL // tn
    return pl.pallas_call(
        _merge_body,
        grid=(t // tm, D_MODEL // tn),
        in_specs=[
            pl.BlockSpec((tm, NSA_Q), lambda i, j: (i, 0)),
            pl.BlockSpec((tm, SSD_INNER), lambda i, j: (i, 0)),
            pl.BlockSpec((tm, RET_V), lambda i, j: (i, 0)),
            pl.BlockSpec((tm, tn), lambda i, j: (i, g0 + j)),
            pl.BlockSpec((tm, tn), lambda i, j: (i, g0 + gstep + j)),
            pl.BlockSpec((tm, tn), lambda i, j: (i, g0 + 2 * gstep + j)),
            pl.BlockSpec((None, NSA_Q, tn), lambda i, j: (layer, 0, j)),
            pl.BlockSpec((None, SSD_INNER, tn), lambda i, j: (layer, 0, j)),
            pl.BlockSpec((None, RET_V, tn), lambda i, j: (layer, 0, j)),
        ],
        out_specs=pl.BlockSpec((tm, tn), lambda i, j: (i, j)),
        out_shape=jax.ShapeDtypeStruct((t, D_MODEL), BF16),
        compiler_params=_params(("parallel", "arbitrary")),
        name="merge",
    )(y_nsa, y_ssd, y_ret, proj, proj, proj, p_nsa, p_ssd, p_ret)


def _mm_res_body(a_ref, w_ref, r_ref, o_ref):
    @pl.when(pl.program_id(2) == 0)
    def _():
        o_ref[...] = r_ref[...]

    o_ref[...] += _dot(a_ref[...], w_ref[...].astype(BF16))


def _mm_res(a, w, res, layer, tk):
    t, k = a.shape
    n = w.shape[2]
    tm, tn = ROW_TILE, 512
    return pl.pallas_call(
        _mm_res_body,
        grid=(t // tm, n // tn, k // tk),
        in_specs=[
            pl.BlockSpec((tm, tk), lambda i, j, kk: (i, kk)),
            pl.BlockSpec((None, tk, tn), lambda i, j, kk: (layer, kk, j)),
            pl.BlockSpec((tm, tn), lambda i, j, kk: (i, j)),
        ],
        out_specs=pl.BlockSpec((tm, tn), lambda i, j, kk: (i, j)),
        out_shape=jax.ShapeDtypeStruct((t, n), F32),
        compiler_params=_params(("parallel", "arbitrary", "arbitrary")),
        name="mm_res",
    )(a, w, res)


def _ffn_up_body(f_ref, wg_ref, wu_ref, o_ref):
    f = f_ref[...]
    gate = _dot(f, wg_ref[...].astype(BF16))
    up = _dot(f, wu_ref[...].astype(BF16))
    o_ref[...] = (jax.nn.silu(gate) * up).astype(o_ref.dtype)


def _ffn_up(f, wg, wu, layer):
    t, d = f.shape
    n = wg.shape[2]
    tm, tn = ROW_TILE, 512
    return pl.pallas_call(
        _ffn_up_body,
        grid=(t // tm, n // tn),
        in_specs=[
            pl.BlockSpec((tm, d), lambda i, j: (i, 0)),
            pl.BlockSpec((None, d, tn), lambda i, j: (layer, 0, j)),
            pl.BlockSpec((None, d, tn), lambda i, j: (layer, 0, j)),
        ],
        out_specs=pl.BlockSpec((tm, tn), lambda i, j: (i, j)),
        out_shape=jax.ShapeDtypeStruct((t, n), BF16),
        compiler_params=_params(("parallel", "arbitrary")),
        name="ffn_up",
    )(f, wg, wu)


def _compress_body(x_ref, pe_ref, w1_ref, w2_ref, o_ref):
    half = (CMP_LEN // 2) * DH
    x = x_ref[...]
    a = _dot(x, w1_ref[:half, :])
    b = _dot(x, w1_ref[half:, :])
    bias = _dot(pe_ref[...], w1_ref[...])[0:1, :]
    nblk = x.shape[0]
    h = a + pltpu.roll(b, nblk - 1, 0) + bias
    h = jax.nn.gelu(h)
    o_ref[...] = _dot(h.astype(BF16), w2_ref[...]).astype(o_ref.dtype)


def _compress(xc, pe, w1, w2):
    b, _, g, nchunk, width = xc.shape
    return pl.pallas_call(
        _compress_body,
        grid=(b, 2, g),
        in_specs=[
            pl.BlockSpec((None, None, None, nchunk, width), lambda bi, m, gi: (bi, m, gi, 0, 0)),
            pl.BlockSpec((None, 8, CMP_LEN * DH), lambda bi, m, gi: (m, 0, 0)),
            pl.BlockSpec((None, CMP_LEN * DH, DH), lambda bi, m, gi: (m, 0, 0)),
            pl.BlockSpec((None, DH, DH), lambda bi, m, gi: (m, 0, 0)),
        ],
        out_specs=pl.BlockSpec((None, None, None, nchunk, DH), lambda bi, m, gi: (bi, m, gi, 0, 0)),
        out_shape=jax.ShapeDtypeStruct((b, 2, g, nchunk, DH), BF16),
        compiler_params=_params(("parallel", "arbitrary", "arbitrary")),
        name="nsa_compress",
    )(xc, pe, w1, w2)


NSA_TQ = 256
KEY_BLK = 128
NSA_COLS = NSA_J * NSA_TQ
SLC_KCHUNK = 256
WIN_SPAN = WINDOW + NSA_TQ
WIN_PAD_BLOCKS = WINDOW // KEY_BLK
AUG = 2 * DH
COL_BLK = SLC_BLOCK // 2
COL_OFF = COL_BLK + 1


def _nsa_body(q_ref, gate_ref, kc_ref, vc_ref, ks_ref, vs_ref, kw_ref, vw_ref, o_ref,
              ksa_ref, kwa_ref, vst_ref, vwt_ref, vct_ref, bw_ref, qsa_ref, qwa_ref, *, seq):
    g = pl.program_id(1)
    i = pl.program_id(2)
    tq = NSA_TQ
    cols = NSA_COLS
    n_slc = seq // SLC_BLOCK
    kb_rows = KEY_BLK
    nkb = seq // kb_rows

    cc = lax.broadcasted_iota(jnp.int32, (1, cols), 1)
    head = cc // tq
    tl = cc - head * tq
    slope_g = jnp.where(g == 0, 0.5, 0.5 ** (NSA_J + 1)).astype(F32)
    slope = slope_g * jnp.where(head == 0, 1.0, jnp.where(head == 1, 0.5, jnp.where(head == 2, 0.25, 0.125)))
    tl_f = tl.astype(F32)

    @pl.when(i == 0)
    def _init():
        def blk_body(kb, carry):
            r0 = pl.multiple_of(kb * kb_rows, kb_rows)
            rows = pl.ds(r0, kb_rows)
            key = r0 + lax.broadcasted_iota(jnp.int32, (kb_rows, DH), 0)
            col = lax.broadcasted_iota(jnp.int32, (kb_rows, DH), 1)
            blk = key // SLC_BLOCK
            off = key - blk * SLC_BLOCK
            extra = jnp.where(col == blk, 1.0, 0.0)
            extra = jnp.where(col == COL_BLK, blk.astype(F32), extra)
            extra = jnp.where(col == COL_OFF, off.astype(F32), extra)
            ksa_ref[rows, 0:DH] = ks_ref[rows, :]
            ksa_ref[rows, DH:AUG] = extra.astype(BF16)
            prow = pl.ds(pl.multiple_of(r0 + WINDOW, kb_rows), kb_rows)
            kwa_ref[prow, 0:DH] = kw_ref[rows, :]
            kwa_ref[prow, DH:AUG] = jnp.zeros((kb_rows, DH), BF16)
            vst_ref[kb] = vs_ref[rows, :].astype(F32).T.astype(BF16)
            vwt_ref[kb + WIN_PAD_BLOCKS] = vw_ref[rows, :].astype(F32).T.astype(BF16)
            return carry

        lax.fori_loop(0, nkb, blk_body, 0)
        pcol = lax.broadcasted_iota(jnp.int32, (WINDOW, DH), 1)
        kwa_ref[0:WINDOW, 0:DH] = jnp.zeros((WINDOW, DH), BF16)
        kwa_ref[0:WINDOW, DH:AUG] = jnp.where(pcol == 0, 1.0, 0.0).astype(BF16)
        vwt_ref[0:WIN_PAD_BLOCKS] = jnp.zeros((WIN_PAD_BLOCKS, DH, kb_rows), BF16)
        vct_ref[...] = vc_ref[...].astype(F32).T.astype(BF16)
        wk = lax.broadcasted_iota(jnp.int32, (WIN_SPAN, cols), 0)
        d = (WINDOW + tl - wk).astype(F32)
        bw_ref[...] = jnp.where((d >= 0) & (d < WINDOW), -slope * d, NEG_INF)

    q = q_ref[...].astype(F32) * (DH ** -0.5)
    qt = jnp.concatenate([q[:, j * DH:(j + 1) * DH].T for j in range(NSA_J)], axis=1).astype(BF16)
    qsa_ref[0:DH, :] = qt
    qwa_ref[0:DH, :] = qt
    er = lax.broadcasted_iota(jnp.int32, (DH, cols), 0)
    qwa_ref[DH:AUG, :] = jnp.where(er == 0, NEG_INF, 0.0).astype(BF16)

    ncmp = kc_ref.shape[0]
    n_idx = lax.broadcasted_iota(jnp.int32, (ncmp, 1), 0)
    rel_end = (n_idx * CMP_STRIDE + (CMP_LEN - 1)).astype(F32) - tl_f
    base = (i * tq).astype(F32)
    s = _dot(kc_ref[...], qt) + slope * rel_end
    s = jnp.where(rel_end <= base, s, NEG_INF)
    p = jnp.exp(s - jnp.max(s, axis=0, keepdims=True))
    p = p / jnp.sum(p, axis=0, keepdims=True)
    p = p * ((base + tl_f) >= (CMP_LEN - 1)).astype(F32)
    o_cmp = _dot(vct_ref[...], p.astype(BF16))

    psum = p[:, 0:tq]
    for j in range(1, NSA_J):
        psum = psum + p[:, j * tq:(j + 1) * tq]
    kk = lax.broadcasted_iota(jnp.int32, (LANE, ncmp), 0)
    nn = lax.broadcasted_iota(jnp.int32, (LANE, ncmp), 1)
    overlap = ((nn * CMP_STRIDE < (kk + 1) * SLC_BLOCK)
               & (nn * CMP_STRIDE + (CMP_LEN - 1) >= kk * SLC_BLOCK)
               & (kk < n_slc))
    imp = _dot_exact_lhs(jnp.where(overlap, 1.0, 0.0).astype(BF16), psum)[0:n_slc, :]
    blk = lax.broadcasted_iota(jnp.int32, (n_slc, tq), 0)
    blk_f = blk.astype(F32)
    cur = (i * tq + lax.broadcasted_iota(jnp.int32, (1, tq), 1)) // SLC_BLOCK
    val = jnp.where(blk > cur, NEG_INF, imp)
    val = jnp.where((blk == cur) | (blk == 0), FORCE_SCORE, val)
    sel = jnp.zeros((n_slc, tq), F32)
    for _ in range(min(SLC_TOPK, n_slc)):
        mx = jnp.max(val, axis=0, keepdims=True)
        idx = jnp.min(jnp.where(val == mx, blk_f, float(n_slc)), axis=0, keepdims=True)
        hit = blk_f == idx
        sel = jnp.where(hit, 1.0, sel)
        val = jnp.where(hit, BELOW_ALL, val)
    pen = (sel - 1.0) * (-NEG_INF)
    pen = jnp.concatenate([pen] * NSA_J, axis=1)
    er2 = lax.broadcasted_iota(jnp.int32, (DH - n_slc, cols), 0) + n_slc
    tail = jnp.where(er2 == COL_BLK, slope * SLC_BLOCK, jnp.where(er2 == COL_OFF, slope, 0.0))
    qsa_ref[DH:AUG, :] = jnp.concatenate([pen, tail], axis=0).astype(BF16)

    hw = 2 * LANE
    nhalf = cols // hw

    def slc_update(sc, c, state):
        m_run, l_run, acc = state
        m_new = jnp.maximum(m_run, jnp.max(sc, axis=0, keepdims=True))
        alpha = jnp.exp(m_run - m_new)
        pr = jnp.exp(sc - m_new)
        l_new = alpha * l_run + jnp.sum(pr, axis=0, keepdims=True)
        prb = pr.astype(BF16)
        per = SLC_KCHUNK // kb_rows
        pv = _dot(vst_ref[per * c], prb[0:kb_rows])
        for kb in range(1, per):
            pv = pv + _dot(vst_ref[per * c + kb], prb[kb * kb_rows:(kb + 1) * kb_rows])
        return m_new, l_new, alpha * acc + pv

    def slc_scores(c, h):
        start = pl.multiple_of(c * SLC_KCHUNK, SLC_KCHUNK)
        return _dot(ksa_ref[pl.ds(start, SLC_KCHUNK), :], qsa_ref[:, h * hw:(h + 1) * hw])

    def slc_step(c, carry):
        scs, states = carry
        nxt = tuple(slc_scores(c + 1, h) for h in range(nhalf))
        return nxt, tuple(slc_update(scs[h], c, states[h]) for h in range(nhalf))

    diag = (i * tq) // SLC_KCHUNK
    state0 = (jnp.full((1, hw), NEG_INF, F32), jnp.zeros((1, hw), F32), jnp.zeros((DH, hw), F32))
    init = (tuple(slc_scores(0, h) for h in range(nhalf)), (state0,) * nhalf)
    scs, states = lax.fori_loop(0, diag, slc_step, init)
    krow = diag * SLC_KCHUNK + lax.broadcasted_iota(jnp.int32, (SLC_KCHUNK, 1), 0)
    o_halves = []
    for h in range(nhalf):
        causal = krow <= i * tq + tl[:, h * hw:(h + 1) * hw]
        _, l_fin, acc_fin = slc_update(jnp.where(causal, scs[h], NEG_INF), diag, states[h])
        o_halves.append(acc_fin / l_fin)
    o_slc = jnp.concatenate(o_halves, axis=1)

    wstart = pl.multiple_of(i * tq, tq)
    sw = _dot(kwa_ref[pl.ds(wstart, WIN_SPAN), :], qwa_ref[...]) + bw_ref[...]
    pw = jnp.exp(sw - jnp.max(sw, axis=0, keepdims=True))
    lw = jnp.sum(pw, axis=0, keepdims=True)
    pwb = pw.astype(BF16)
    wblk = i * (tq // kb_rows)
    o_win = _dot(vwt_ref[wblk], pwb[0:kb_rows])
    for kb in range(1, WIN_SPAN // kb_rows):
        o_win = o_win + _dot(vwt_ref[wblk + kb], pwb[kb * kb_rows:(kb + 1) * kb_rows])
    o_win = o_win / lw

    gt = jax.nn.sigmoid(gate_ref[...].astype(F32)).T

    def gate_row(branch):
        return jnp.concatenate([gt[3 * j + branch:3 * j + branch + 1, :] for j in range(NSA_J)], axis=1)

    out_t = gate_row(0) * o_cmp + gate_row(1) * o_slc + gate_row(2) * o_win
    o_ref[...] = jnp.concatenate([out_t[:, j * tq:(j + 1) * tq].T for j in range(NSA_J)], axis=1).astype(o_ref.dtype)


def _nsa(proj, cmp, batch, seq):
    t = proj.shape[0]
    nq = seq // NSA_TQ
    qw = NSA_J * DH
    kv0 = OFF_KV // DH

    def kv_spec(which):
        return pl.BlockSpec((seq, DH), lambda b, g, i: (b, kv0 + 2 * which + g))

    return pl.pallas_call(
        functools.partial(_nsa_body, seq=seq),
        grid=(batch, NSA_GROUPS, nq),
        in_specs=[
            pl.BlockSpec((NSA_TQ, qw), lambda b, g, i: (b * nq + i, g)),
            pl.BlockSpec((NSA_TQ, LANE), lambda b, g, i: (b * nq + i, OFF_NG // LANE + g)),
            pl.BlockSpec((None, None, None, seq // CMP_STRIDE, DH), lambda b, g, i: (b, 0, g, 0, 0)),
            pl.BlockSpec((None, None, None, seq // CMP_STRIDE, DH), lambda b, g, i: (b, 1, g, 0, 0)),
            kv_spec(2), kv_spec(3), kv_spec(4), kv_spec(5),
        ],
        out_specs=pl.BlockSpec((NSA_TQ, qw), lambda b, g, i: (b * nq + i, g)),
        out_shape=jax.ShapeDtypeStruct((t, NSA_Q), BF16),
        scratch_shapes=[
            pltpu.VMEM((seq, AUG), BF16),
            pltpu.VMEM((seq + WINDOW, AUG), BF16),
            pltpu.VMEM((seq // KEY_BLK, DH, KEY_BLK), BF16),
            pltpu.VMEM((seq // KEY_BLK + WIN_PAD_BLOCKS, DH, KEY_BLK), BF16),
            pltpu.VMEM((DH, seq // CMP_STRIDE), BF16),
            pltpu.VMEM((WIN_SPAN, NSA_COLS), F32),
            pltpu.VMEM((AUG, NSA_COLS), BF16),
            pltpu.VMEM((AUG, NSA_COLS), BF16),
        ],
        compiler_params=_params(("parallel", "arbitrary", "arbitrary")),
        name="nsa_attn",
    )(proj, proj, cmp, cmp, proj, proj, proj, proj)


def _causal_conv(raw_ref, prev_ref, w_ref, b_ref, first):
    cur = raw_ref[...].astype(F32)

    @pl.when(first)
    def _():
        prev_ref[...] = jnp.zeros_like(prev_ref)

    prev = prev_ref[...]
    row = lax.broadcasted_iota(jnp.int32, cur.shape, 0)
    out = b_ref[...] + w_ref[SSD_CONV - 1:SSD_CONV, :] * cur
    for back in range(1, SSD_CONV):
        shifted = jnp.where(row < back, pltpu.roll(prev, back, 0), pltpu.roll(cur, back, 0))
        out = out + w_ref[SSD_CONV - 1 - back:SSD_CONV - back, :] * shifted
    prev_ref[...] = cur
    return jax.nn.silu(out)


def _ssd_body(x_ref, b_ref, c_ref, z_ref, dt_ref, cwx_ref, cwb_ref, cwc_ref, cbx_ref, cbb_ref, cbc_ref,
              dtb_ref, alog_ref, dskip_ref, nw_ref, o_ref, state_ref, px_ref, pb_ref, pc_ref):
    first = pl.program_id(2) == 0
    lc = CHUNK
    width = SSD_HG * SSD_P

    @pl.when(first)
    def _():
        state_ref[...] = jnp.zeros_like(state_ref)

    xs = _causal_conv(x_ref, px_ref, cwx_ref, cbx_ref, first)
    bm = _causal_conv(b_ref, pb_ref, cwb_ref, cbb_ref, first)
    cm = _causal_conv(c_ref, pc_ref, cwc_ref, cbc_ref, first)

    dt = jax.nn.softplus(dt_ref[...].astype(F32) + dtb_ref[...])
    a = dt * -jnp.exp(alog_ref[...])
    li = lax.broadcasted_iota(jnp.int32, (lc, lc), 0)
    si = lax.broadcasted_iota(jnp.int32, (lc, lc), 1)
    causal = li >= si
    tril = jnp.where(causal, 1.0, 0.0).astype(BF16)
    a_cum = _dot_exact_lhs(tril, a)
    a_cum_t = a_cum.T

    eh = lax.broadcasted_iota(jnp.int32, (LANE, width), 0)
    ec = lax.broadcasted_iota(jnp.int32, (LANE, width), 1) // SSD_P
    spread = jnp.where(eh == ec, 1.0, 0.0).astype(BF16)
    dt_x = _dot_exact_rhs(dt, spread)
    acum_x = _dot_exact_rhs(a_cum, spread)
    last_x = acum_x[lc - 1:lc, :]

    xdt = xs * dt_x
    xdt_bf = xdt.astype(BF16)
    cm_bf = cm.astype(BF16)
    cb = _dot_nt(cm_bf, bm.astype(BF16))

    y_heads = []
    for h in range(SSD_HG):
        col = acum_x[:, h * SSD_P:h * SSD_P + 1]
        rowv = a_cum_t[h:h + 1, :]
        decay = jnp.exp(jnp.where(causal, col - rowv, -jnp.inf))
        y_heads.append(_dot((cb * decay).astype(BF16), xdt_bf[:, h * SSD_P:(h + 1) * SSD_P]))
    y_diag = jnp.concatenate(y_heads, axis=-1)

    state = state_ref[...]
    y_off = _dot(cm_bf, state.astype(BF16)) * jnp.exp(acum_x)
    to_end = jnp.exp(last_x - acum_x)
    chunk_state = _dot(bm.T.astype(BF16), (xdt * to_end).astype(BF16))
    state_ref[...] = state * jnp.exp(last_x) + chunk_state

    y = y_diag + y_off + xs * dskip_ref[...]
    y = y * jax.nn.silu(z_ref[...].astype(F32))
    y = y * lax.rsqrt(jnp.mean(y * y, axis=-1, keepdims=True) + EPS)
    o_ref[...] = (y * nw_ref[...]).astype(o_ref.dtype)


def _ssd(proj, lw, layer, batch, seq):
    t = proj.shape[0]
    nc = seq // CHUNK
    width = SSD_HG * SSD_P
    x0 = OFF_XBC // width
    b0 = (OFF_XBC + SSD_INNER) // SSD_N
    c0 = b0 + SSD_GROUPS
    z0 = OFF_Z // width
    d0 = OFF_DT // LANE
    nblk = SSD_INNER // SSD_N

    def row(b, g, c):
        return b * nc + c

    return pl.pallas_call(
        _ssd_body,
        grid=(batch, SSD_GROUPS, nc),
        in_specs=[
            pl.BlockSpec((CHUNK, width), lambda b, g, c: (row(b, g, c), x0 + g)),
            pl.BlockSpec((CHUNK, SSD_N), lambda b, g, c: (row(b, g, c), b0 + g)),
            pl.BlockSpec((CHUNK, SSD_N), lambda b, g, c: (row(b, g, c), c0 + g)),
            pl.BlockSpec((CHUNK, width), lambda b, g, c: (row(b, g, c), z0 + g)),
            pl.BlockSpec((CHUNK, LANE), lambda b, g, c: (row(b, g, c), d0 + g)),
            pl.BlockSpec((None, SSD_CONV, width), lambda b, g, c: (layer, 0, g)),
            pl.BlockSpec((None, SSD_CONV, SSD_N), lambda b, g, c: (layer, 0, nblk + g)),
            pl.BlockSpec((None, SSD_CONV, SSD_N), lambda b, g, c: (layer, 0, nblk + SSD_GROUPS + g)),
            pl.BlockSpec((None, 1, width), lambda b, g, c: (layer, 0, g)),
            pl.BlockSpec((None, 1, SSD_N), lambda b, g, c: (layer, 0, nblk + g)),
            pl.BlockSpec((None, 1, SSD_N), lambda b, g, c: (layer, 0, nblk + SSD_GROUPS + g)),
            pl.BlockSpec((None, None, 1, LANE), lambda b, g, c: (layer, g, 0, 0)),
            pl.BlockSpec((None, None, 1, LANE), lambda b, g, c: (layer, g, 0, 0)),
            pl.BlockSpec((None, 1, width), lambda b, g, c: (layer, 0, g)),
            pl.BlockSpec((None, 1, width), lambda b, g, c: (layer, 0, g)),
        ],
        out_specs=pl.BlockSpec((CHUNK, width), lambda b, g, c: (row(b, g, c), g)),
        out_shape=jax.ShapeDtypeStruct((t, SSD_INNER), BF16),
        scratch_shapes=[
            pltpu.VMEM((SSD_N, width), F32),
            pltpu.VMEM((CHUNK, width), F32),
            pltpu.VMEM((CHUNK, SSD_N), F32),
            pltpu.VMEM((CHUNK, SSD_N), F32),
        ],
        compiler_params=_params(("parallel", "arbitrary", "arbitrary")),
        name="ssd",
    )(proj, proj, proj, proj, proj, lw["conv_w"], lw["conv_w"], lw["conv_w"], lw["conv_b"], lw["conv_b"],
      lw["conv_b"], lw["dt_bias"], lw["a_log"], lw["d_skip"], lw["ssd_norm"])


def _ret_log_gamma(h):
    return float(np.log1p(-np.exp2(-5.0 - h)))


def _ret_body(q_ref, k_ref, v0_ref, v1_ref, g0_ref, g1_ref, nw_ref, o_ref, state_ref):
    lc = CHUNK
    half = RET_HEADS // 2
    v_refs = (v0_ref, v1_ref)
    g_refs = (g0_ref, g1_ref)

    @pl.when(pl.program_id(1) == 0)
    def _():
        state_ref[...] = jnp.zeros_like(state_ref)

    li = lax.broadcasted_iota(jnp.int32, (lc, lc), 0)
    si = lax.broadcasted_iota(jnp.int32, (lc, lc), 1)
    diff = (li - si).astype(F32)
    idx = lax.broadcasted_iota(jnp.int32, (lc, 1), 0).astype(F32)

    for h in range(RET_HEADS):
        log_g = _ret_log_gamma(h)
        dmat = jnp.where(diff >= 0, jnp.exp(diff * log_g), 0.0)
        k_dec = jnp.exp((lc - 1.0 - idx) * log_g)
        q_dec = jnp.exp((idx + 1.0) * log_g)
        q = q_ref[:, h * RET_DK:(h + 1) * RET_DK].astype(F32) * (RET_DK ** -0.5)
        k = k_ref[:, h * RET_DK:(h + 1) * RET_DK].astype(F32)
        vsl = slice((h % half) * RET_DV, (h % half + 1) * RET_DV)
        v = v_refs[h // half][:, vsl]
        scores = _dot_nt(q.astype(BF16), k.astype(BF16)) * dmat
        state = state_ref[h]
        y = _dot(scores.astype(BF16), v) + _dot((q * q_dec).astype(BF16), state.astype(BF16))
        chunk_kv = _dot((k * k_dec).T.astype(BF16), v)
        state_ref[h] = state * float(np.exp(lc * log_g)) + chunk_kv

        mu = jnp.mean(y, axis=-1, keepdims=True)
        yc = y - mu
        var = jnp.mean(yc * yc, axis=-1, keepdims=True)
        y = yc * lax.rsqrt(var + EPS) * nw_ref[h]
        gate = jax.nn.silu(g_refs[h // half][:, vsl].astype(F32))
        o_ref[:, h * RET_DV:(h + 1) * RET_DV] = (gate * y).astype(o_ref.dtype)


def _retention(proj, nw, layer, batch, seq):
    t = proj.shape[0]
    nc = seq // CHUNK
    hw = RET_V // 2
    q0 = OFF_RQK // RET_QK
    v0 = OFF_RV // hw
    g0 = OFF_RG // hw
    return pl.pallas_call(
        _ret_body,
        grid=(batch, nc),
        in_specs=[
            pl.BlockSpec((CHUNK, RET_QK), lambda b, c: (b * nc + c, q0)),
            pl.BlockSpec((CHUNK, RET_QK), lambda b, c: (b * nc + c, q0 + 1)),
            pl.BlockSpec((CHUNK, hw), lambda b, c: (b * nc + c, v0)),
            pl.BlockSpec((CHUNK, hw), lambda b, c: (b * nc + c, v0 + 1)),
            pl.BlockSpec((CHUNK, hw), lambda b, c: (b * nc + c, g0)),
            pl.BlockSpec((CHUNK, hw), lambda b, c: (b * nc + c, g0 + 1)),
            pl.BlockSpec((None, RET_HEADS, 1, RET_DV), lambda b, c: (layer, 0, 0, 0)),
        ],
        out_specs=pl.BlockSpec((CHUNK, RET_V), lambda b, c: (b * nc + c, 0)),
        out_shape=jax.ShapeDtypeStruct((t, RET_V), BF16),
        scratch_shapes=[pltpu.VMEM((RET_HEADS, RET_DK, RET_DV), F32)],
        compiler_params=_params(("parallel", "arbitrary")),
        name="retention",
    )(proj, proj, proj, proj, proj, proj, nw)


def _per_group_cols(seg, groups):
    lead = seg.shape[:-1]
    per = seg.shape[-1] // groups
    seg = seg.reshape(*lead, groups, per)
    seg = jnp.pad(seg, [(0, 0)] * (seg.ndim - 1) + [(0, LANE - per)])
    return seg.reshape(*lead, groups * LANE)


def _small_w_in(w_in):
    o_ng = IN_SPLITS[0] + IN_SPLITS[1]
    o_dt = o_ng + IN_SPLITS[2] + IN_SPLITS[3] + IN_SPLITS[4]
    ng = _per_group_cols(w_in[..., o_ng:o_ng + IN_SPLITS[2]], NSA_GROUPS)
    dt = _per_group_cols(w_in[..., o_dt:o_dt + IN_SPLITS[5]], SSD_GROUPS)
    pad = jnp.zeros((*w_in.shape[:-1], N_SMALL - ng.shape[-1] - dt.shape[-1]), w_in.dtype)
    return jnp.concatenate([ng, dt, pad], axis=-1).astype(BF16)


def _per_group_lanes(v, groups):
    depth, heads = v.shape
    per = heads // groups
    v = v.reshape(depth, groups, 1, per)
    return jnp.pad(v, [(0, 0), (0, 0), (0, 0), (0, LANE - per)])


def _layer(x, lw, layer, batch, seq):
    u = _rms(x, lw["norm_mix"], layer, BF16)
    proj = _inproj(u, lw["w_in"], lw["w_small"], layer)

    kv = proj[:, OFF_KV:OFF_KV + 2 * NSA_KV].reshape(batch, seq // CMP_STRIDE, CMP_STRIDE, 2, NSA_GROUPS, DH)
    xc = kv.transpose(0, 3, 4, 1, 2, 5).reshape(batch, 2, NSA_GROUPS, seq // CMP_STRIDE, CMP_STRIDE * DH)
    cmp = _compress(xc, lw["cmp_pe"][layer], lw["cmp_w1"][layer], lw["cmp_w2"][layer])

    y_nsa = _nsa(proj, cmp, batch, seq)
    y_ssd = _ssd(proj, lw, layer, batch, seq)
    y_ret = _retention(proj, lw["ret_norm"], layer, batch, seq)

    merged = _merge(y_nsa, y_ssd, y_ret, proj, lw["p_nsa"], lw["p_ssd"], lw["p_ret"], layer)
    x = _mm_res(merged, lw["w_out"], x, layer, D_MODEL)
    f = _rms(x, lw["norm_ffn"], layer, BF16)
    hidden = _ffn_up(f, lw["w_gate"], lw["w_up"], layer)
    return _mm_res(hidden, lw["w_down"], x, layer, D_FF // 4)


def kernel(x, norm_mix, w_in, cmp_k_pe, cmp_k_w1, cmp_k_w2, cmp_v_pe, cmp_v_w1, cmp_v_w2, conv_w, conv_b, dt_bias, a_log, d_skip, ssd_norm, ret_norm, p_nsa, p_ssd, p_ret, w_out, norm_ffn, w_gate, w_up, w_down, norm_final):
    batch, seq, d = x.shape
    depth = w_in.shape[0]
    pe = jnp.stack([cmp_k_pe, cmp_v_pe], axis=1).reshape(depth, 2, 1, CMP_LEN * DH)
    lw = {
        "norm_mix": norm_mix[:, None, :],
        "w_in": w_in,
        "w_small": _small_w_in(w_in),
        "cmp_pe": jnp.pad(pe, [(0, 0), (0, 0), (0, 7), (0, 0)]).astype(BF16),
        "cmp_w1": jnp.stack([cmp_k_w1, cmp_v_w1], axis=1).astype(BF16),
        "cmp_w2": jnp.stack([cmp_k_w2, cmp_v_w2], axis=1).astype(BF16),
        "conv_w": conv_w,
        "conv_b": conv_b[:, None, :],
        "dt_bias": _per_group_lanes(dt_bias, SSD_GROUPS),
        "a_log": _per_group_lanes(a_log, SSD_GROUPS),
        "d_skip": jnp.repeat(d_skip, SSD_P, axis=-1)[:, None, :],
        "ssd_norm": ssd_norm[:, None, :],
        "ret_norm": ret_norm[:, :, None, :],
        "p_nsa": p_nsa, "p_ssd": p_ssd, "p_ret": p_ret, "w_out": w_out,
        "norm_ffn": norm_ffn[:, None, :],
        "w_gate": w_gate, "w_up": w_up, "w_down": w_down,
    }
    h = x.reshape(batch * seq, d)
    for layer in range(depth):
        h = _layer(h, lw, layer, batch, seq)
    return _rms(h, norm_final[None, None, :], 0, F32).reshape(batch, seq, d)
```

```python
import functools

import jax
import jax.numpy as jnp
import numpy as np
from jax import lax
from jax.experimental import pallas as pl
from jax.experimental.pallas import tpu as pltpu

F32 = jnp.float32
BF16 = jnp.bfloat16

D_MODEL = 2048
NSA_HEADS = 8
NSA_GROUPS = 2
NSA_J = NSA_HEADS // NSA_GROUPS
DH = 128
CMP_LEN = 32
CMP_STRIDE = 16
SLC_BLOCK = 64
SLC_TOPK = 8
WINDOW = 512
SSD_INNER = D_MODEL
SSD_P = 64
SSD_HEADS = SSD_INNER // SSD_P
SSD_GROUPS = 4
SSD_HG = SSD_HEADS // SSD_GROUPS
SSD_N = 128
SSD_CONV = 4
CHUNK = 128
RET_HEADS = 4
RET_DK = 128
RET_DV = 256
D_FF = -(-8 * D_MODEL // (3 * 256)) * 256
NSA_Q = NSA_HEADS * DH
NSA_KV = NSA_GROUPS * DH
SSD_BC = SSD_GROUPS * SSD_N
SSD_CONV_DIM = SSD_INNER + 2 * SSD_BC
RET_QK = RET_HEADS * RET_DK
RET_V = RET_HEADS * RET_DV
IN_SPLITS = (NSA_Q, 6 * NSA_KV, 3 * NSA_HEADS, SSD_INNER, SSD_CONV_DIM, SSD_HEADS,
             2 * RET_QK, RET_V, RET_V, 3 * D_MODEL)
D_IN = sum(IN_SPLITS)
NEG_INF = -1e30
FORCE_SCORE = 1e9
BELOW_ALL = -3e38
EPS = 1e-6

LANE = 128

OFF_Q = 0
OFF_KV = OFF_Q + NSA_Q
OFF_Z = OFF_KV + 6 * NSA_KV
OFF_XBC = OFF_Z + SSD_INNER
OFF_RQK = OFF_XBC + SSD_CONV_DIM
OFF_RV = OFF_RQK + 2 * RET_QK
OFF_RG = OFF_RV + RET_V
OFF_MG = OFF_RG + RET_V
OFF_NG = OFF_MG + 3 * D_MODEL
OFF_DT = OFF_NG + NSA_GROUPS * LANE
N_USED = OFF_DT + SSD_GROUPS * LANE
PROJ_TN = 512
N_PACK = -(-N_USED // PROJ_TN) * PROJ_TN
N_MAIN_BLOCKS = OFF_NG // PROJ_TN
N_SMALL = N_PACK - OFF_NG

SHIFT_B = IN_SPLITS[2]
SHIFT_C = IN_SPLITS[2] + IN_SPLITS[5]
BLK_B = OFF_Z // PROJ_TN
BLK_C = OFF_RQK // PROJ_TN

ROW_TILE = 2048
VMEM_LIMIT = 56 * 1024 * 1024


def _dot(a, b):
    return jnp.dot(a, b, preferred_element_type=F32)


def _dot_nt(a, b):
    return lax.dot_general(a, b, (((1,), (1,)), ((), ())), preferred_element_type=F32)


def _split3(x):
    hi = x.astype(BF16)
    r1 = x - hi.astype(F32)
    mid = r1.astype(BF16)
    lo = (r1 - mid.astype(F32)).astype(BF16)
    return hi, mid, lo


def _dot_exact_rhs(x, m_bf16):
    hi, mid, lo = _split3(x)
    return _dot(hi, m_bf16) + _dot(mid, m_bf16) + _dot(lo, m_bf16)


def _dot_exact_lhs(m_bf16, x):
    hi, mid, lo = _split3(x)
    return _dot(m_bf16, hi) + _dot(m_bf16, mid) + _dot(m_bf16, lo)


def _params(sem):
    return pltpu.CompilerParams(dimension_semantics=sem, vmem_limit_bytes=VMEM_LIMIT)


def _rms_body(x_ref, nw_ref, o_ref):
    slab = 128

    def body(r, carry):
        sl = pl.ds(pl.multiple_of(r * slab, slab), slab)
        x = x_ref[sl, :]
        ms = jnp.mean(x * x, axis=-1, keepdims=True)
        o_ref[sl, :] = (x * lax.rsqrt(ms + EPS) * nw_ref[...]).astype(o_ref.dtype)
        return carry

    lax.fori_loop(0, x_ref.shape[0] // slab, body, 0)


def _rms(x, nw, layer, out_dtype):
    t, d = x.shape
    tm = 1024
    return pl.pallas_call(
        _rms_body,
        grid=(t // tm,),
        in_specs=[pl.BlockSpec((tm, d), lambda i: (i, 0)),
                  pl.BlockSpec((None, 1, d), lambda i: (layer, 0, 0))],
        out_specs=pl.BlockSpec((tm, d), lambda i: (i, 0)),
        out_shape=jax.ShapeDtypeStruct((t, d), out_dtype),
        compiler_params=_params(("parallel",)),
        name="rmsnorm",
    )(x, nw)


def _inproj_body(u_ref, w_ref, ws_ref, o_ref):
    j = pl.program_id(1)

    @pl.when(j < N_MAIN_BLOCKS)
    def _():
        o_ref[...] = _dot_nt(u_ref[...], w_ref[0].astype(BF16)).astype(o_ref.dtype)

    @pl.when(j >= N_MAIN_BLOCKS)
    def _():
        o_ref[...] = _dot_nt(u_ref[...], ws_ref[...]).astype(o_ref.dtype)


def _inproj(u, w_t, w_small_t, layer):
    t, d = u.shape
    tm, tn = ROW_TILE, PROJ_TN
    last_main = N_MAIN_BLOCKS - 1

    def w_index(i, j):
        jc = jnp.minimum(j, last_main)
        shift = jnp.where(jc >= BLK_C, SHIFT_C, jnp.where(jc >= BLK_B, SHIFT_B, 0))
        return (layer, pl.multiple_of(jc * tn + shift, 8), 0)

    return pl.pallas_call(
        _inproj_body,
        grid=(t // tm, N_PACK // tn),
        in_specs=[
            pl.BlockSpec((tm, d), lambda i, j: (i, 0)),
            pl.BlockSpec((pl.Element(1), pl.Element(tn), pl.Element(d)), w_index),
            pl.BlockSpec((None, tn, d), lambda i, j: (layer, jnp.maximum(j - N_MAIN_BLOCKS, 0), 0)),
        ],
        out_specs=pl.BlockSpec((tm, tn), lambda i, j: (i, j)),
        out_shape=jax.ShapeDtypeStruct((t, N_PACK), BF16),
        compiler_params=_params(("parallel", "arbitrary")),
        name="inproj",
    )(u, w_t, w_small_t)


def _merge_body(yn_ref, ys_ref, yr_ref, gn_ref, gs_ref, gr_ref, pn_ref, ps_ref, pr_ref, o_ref):
    def gate(g_ref):
        return jax.nn.sigmoid(g_ref[...].astype(F32))

    acc = gate(gn_ref) * _dot(yn_ref[...], pn_ref[...].astype(BF16))
    acc = acc + gate(gs_ref) * _dot(ys_ref[...], ps_ref[...].astype(BF16))
    acc = acc + gate(gr_ref) * _dot(yr_ref[...], pr_ref[...].astype(BF16))
    o_ref[...] = acc.astype(o_ref.dtype)


def _merge(y_nsa, y_ssd, y_ret, proj, p_nsa, p_ssd, p_ret, layer):
    t = y_nsa.shape[0]
    tm, tn = ROW_TILE, 256
    g0 = OFF_MG // tn
    gstep = D_MODEL // tn
    once = pl.Buffered(1)
    return pl.pallas_call(
        _merge_body,
        grid=(t // tm, D_MODEL // tn),
        in_specs=[
            pl.BlockSpec((tm, NSA_Q), lambda i, j: (i, 0), pipeline_mode=once),
            pl.BlockSpec((tm, SSD_INNER), lambda i, j: (i, 0), pipeline_mode=once),
            pl.BlockSpec((tm, RET_V), lambda i, j: (i, 0), pipeline_mode=once),
            pl.BlockSpec((tm, tn), lambda i, j: (i, g0 + j)),
            pl.BlockSpec((tm, tn), lambda i, j: (i, g0 + gstep + j)),
            pl.BlockSpec((tm, tn), lambda i, j: (i, g0 + 2 * gstep + j)),
            pl.BlockSpec((None, NSA_Q, tn), lambda i, j: (layer, 0, j)),
            pl.BlockSpec((None, SSD_INNER, tn), lambda i, j: (layer, 0, j)),
            pl.BlockSpec((None, RET_V, tn), lambda i, j: (layer, 0, j)),
        ],
        out_specs=pl.BlockSpec((tm, tn), lambda i, j: (i, j)),
        out_shape=jax.ShapeDtypeStruct((t, D_MODEL), BF16),
        compiler_params=_params(("parallel", "arbitrary")),
        name="merge",
    )(y_nsa, y_ssd, y_ret, proj, proj, proj, p_nsa, p_ssd, p_ret)


def _mm_res_body(a_ref, w_ref, r_ref, o_ref):
    o_ref[...] = r_ref[...] + _dot(a_ref[...], w_ref[...].astype(BF16))


def _mm_res(a, w, res, layer, tn):
    t, k = a.shape
    n = w.shape[2]
    tm = ROW_TILE
    return pl.pallas_call(
        _mm_res_body,
        grid=(t // tm, n // tn),
        in_specs=[
            pl.BlockSpec((tm, k), lambda i, j: (i, 0), pipeline_mode=pl.Buffered(1)),
            pl.BlockSpec((None, k, tn), lambda i, j: (layer, 0, j)),
            pl.BlockSpec((tm, tn), lambda i, j: (i, j)),
        ],
        out_specs=pl.BlockSpec((tm, tn), lambda i, j: (i, j)),
        out_shape=jax.ShapeDtypeStruct((t, n), F32),
        compiler_params=_params(("parallel", "arbitrary")),
        name="mm_res",
    )(a, w, res)


def _ffn_up_body(f_ref, wg_ref, wu_ref, o_ref):
    f = f_ref[...]
    gate = _dot(f, wg_ref[...].astype(BF16))
    up = _dot(f, wu_ref[...].astype(BF16))
    o_ref[...] = (jax.nn.silu(gate) * up).astype(o_ref.dtype)


def _ffn_up(f, wg, wu, layer):
    t, d = f.shape
    n = wg.shape[2]
    tm, tn = ROW_TILE, 512
    return pl.pallas_call(
        _ffn_up_body,
        grid=(t // tm, n // tn),
        in_specs=[
            pl.BlockSpec((tm, d), lambda i, j: (i, 0)),
            pl.BlockSpec((None, d, tn), lambda i, j: (layer, 0, j)),
            pl.BlockSpec((None, d, tn), lambda i, j: (layer, 0, j)),
        ],
        out_specs=pl.BlockSpec((tm, tn), lambda i, j: (i, j)),
        out_shape=jax.ShapeDtypeStruct((t, n), BF16),
        compiler_params=_params(("parallel", "arbitrary")),
        name="ffn_up",
    )(f, wg, wu)


def _compress_body(x_ref, pe_ref, w1_ref, w2_ref, o_ref):
    half = (CMP_LEN // 2) * DH
    x = x_ref[...]
    a = _dot(x, w1_ref[:half, :])
    b = _dot(x, w1_ref[half:, :])
    bias = _dot(pe_ref[...], w1_ref[...])[0:1, :]
    nblk = x.shape[0]
    h = a + pltpu.roll(b, nblk - 1, 0) + bias
    h = jax.nn.gelu(h)
    o_ref[...] = _dot(h.astype(BF16), w2_ref[...]).astype(o_ref.dtype)


def _compress(xc, pe, w1, w2):
    b, _, g, nchunk, width = xc.shape
    return pl.pallas_call(
        _compress_body,
        grid=(b, 2, g),
        in_specs=[
            pl.BlockSpec((None, None, None, nchunk, width), lambda bi, m, gi: (bi, m, gi, 0, 0)),
            pl.BlockSpec((None, 8, CMP_LEN * DH), lambda bi, m, gi: (m, 0, 0)),
            pl.BlockSpec((None, CMP_LEN * DH, DH), lambda bi, m, gi: (m, 0, 0)),
            pl.BlockSpec((None, DH, DH), lambda bi, m, gi: (m, 0, 0)),
        ],
        out_specs=pl.BlockSpec((None, None, None, nchunk, DH), lambda bi, m, gi: (bi, m, gi, 0, 0)),
        out_shape=jax.ShapeDtypeStruct((b, 2, g, nchunk, DH), BF16),
        compiler_params=_params(("parallel", "arbitrary", "arbitrary")),
        name="nsa_compress",
    )(xc, pe, w1, w2)


NSA_TQ = 256
KEY_BLK = 128
NSA_COLS = NSA_J * NSA_TQ
SLC_KCHUNK = 256
WIN_SPAN = WINDOW + NSA_TQ
WIN_PAD_BLOCKS = WINDOW // KEY_BLK
AUG = 2 * DH
COL_BLK = SLC_BLOCK // 2
COL_OFF = COL_BLK + 1


def _nsa_body(q_ref, gate_ref, kc_ref, vc_ref, ks_ref, vs_ref, kw_ref, vw_ref, o_ref,
              ksa_ref, kwa_ref, vst_ref, vwt_ref, vct_ref, bw_ref, qsa_ref, qwa_ref, *, seq):
    g = pl.program_id(1)
    i = pl.program_id(2)
    tq = NSA_TQ
    cols = NSA_COLS
    n_slc = seq // SLC_BLOCK
    kb_rows = KEY_BLK
    nkb = seq // kb_rows

    cc = lax.broadcasted_iota(jnp.int32, (1, cols), 1)
    head = cc // tq
    tl = cc - head * tq
    slope_g = jnp.where(g == 0, 0.5, 0.5 ** (NSA_J + 1)).astype(F32)
    slope = slope_g * jnp.where(head == 0, 1.0, jnp.where(head == 1, 0.5, jnp.where(head == 2, 0.25, 0.125)))
    tl_f = tl.astype(F32)

    @pl.when(i == 0)
    def _init():
        def blk_body(kb, carry):
            r0 = pl.multiple_of(kb * kb_rows, kb_rows)
            rows = pl.ds(r0, kb_rows)
            key = r0 + lax.broadcasted_iota(jnp.int32, (kb_rows, DH), 0)
            col = lax.broadcasted_iota(jnp.int32, (kb_rows, DH), 1)
            blk = key // SLC_BLOCK
            off = key - blk * SLC_BLOCK
            extra = jnp.where(col == blk, 1.0, 0.0)
            extra = jnp.where(col == COL_BLK, blk.astype(F32), extra)
            extra = jnp.where(col == COL_OFF, off.astype(F32), extra)
            ksa_ref[rows, 0:DH] = ks_ref[rows, :]
            ksa_ref[rows, DH:AUG] = extra.astype(BF16)
            prow = pl.ds(pl.multiple_of(r0 + WINDOW, kb_rows), kb_rows)
            kwa_ref[prow, 0:DH] = kw_ref[rows, :]
            kwa_ref[prow, DH:AUG] = jnp.zeros((kb_rows, DH), BF16)
            vst_ref[kb] = vs_ref[rows, :].astype(F32).T.astype(BF16)
            vwt_ref[kb + WIN_PAD_BLOCKS] = vw_ref[rows, :].astype(F32).T.astype(BF16)
            return carry

        lax.fori_loop(0, nkb, blk_body, 0)
        pcol = lax.broadcasted_iota(jnp.int32, (WINDOW, DH), 1)
        kwa_ref[0:WINDOW, 0:DH] = jnp.zeros((WINDOW, DH), BF16)
        kwa_ref[0:WINDOW, DH:AUG] = jnp.where(pcol == 0, 1.0, 0.0).astype(BF16)
        vwt_ref[0:WIN_PAD_BLOCKS] = jnp.zeros((WIN_PAD_BLOCKS, DH, kb_rows), BF16)
        vct_ref[...] = vc_ref[...].astype(F32).T.astype(BF16)
        wk = lax.broadcasted_iota(jnp.int32, (WIN_SPAN, cols), 0)
        d = (WINDOW + tl - wk).astype(F32)
        bw_ref[...] = jnp.where((d >= 0) & (d < WINDOW), -slope * d, NEG_INF)

    q = q_ref[...].astype(F32) * (DH ** -0.5)
    qt = jnp.concatenate([q[:, j * DH:(j + 1) * DH].T for j in range(NSA_J)], axis=1).astype(BF16)
    qsa_ref[0:DH, :] = qt
    qwa_ref[0:DH, :] = qt
    er = lax.broadcasted_iota(jnp.int32, (DH, cols), 0)
    qwa_ref[DH:AUG, :] = jnp.where(er == 0, NEG_INF, 0.0).astype(BF16)

    ncmp = kc_ref.shape[0]
    n_idx = lax.broadcasted_iota(jnp.int32, (ncmp, 1), 0)
    rel_end = (n_idx * CMP_STRIDE + (CMP_LEN - 1)).astype(F32) - tl_f
    base = (i * tq).astype(F32)
    s = _dot(kc_ref[...], qt) + slope * rel_end
    s = jnp.where(rel_end <= base, s, NEG_INF)
    p = jnp.exp(s - jnp.max(s, axis=0, keepdims=True))
    p = p / jnp.sum(p, axis=0, keepdims=True)
    p = p * ((base + tl_f) >= (CMP_LEN - 1)).astype(F32)
    o_cmp = _dot(vct_ref[...], p.astype(BF16))

    psum = p[:, 0:tq]
    for j in range(1, NSA_J):
        psum = psum + p[:, j * tq:(j + 1) * tq]
    kk = lax.broadcasted_iota(jnp.int32, (LANE, ncmp), 0)
    nn = lax.broadcasted_iota(jnp.int32, (LANE, ncmp), 1)
    overlap = ((nn * CMP_STRIDE < (kk + 1) * SLC_BLOCK)
               & (nn * CMP_STRIDE + (CMP_LEN - 1) >= kk * SLC_BLOCK)
               & (kk < n_slc))
    imp = _dot_exact_lhs(jnp.where(overlap, 1.0, 0.0).astype(BF16), psum)[0:n_slc, :]
    blk = lax.broadcasted_iota(jnp.int32, (n_slc, tq), 0)
    blk_f = blk.astype(F32)
    cur = (i * tq + lax.broadcasted_iota(jnp.int32, (1, tq), 1)) // SLC_BLOCK
    val = jnp.where(blk > cur, NEG_INF, imp)
    val = jnp.where((blk == cur) | (blk == 0), FORCE_SCORE, val)
    sel = jnp.zeros((n_slc, tq), F32)
    for _ in range(min(SLC_TOPK, n_slc)):
        mx = jnp.max(val, axis=0, keepdims=True)
        idx = jnp.min(jnp.where(val == mx, blk_f, float(n_slc)), axis=0, keepdims=True)
        hit = blk_f == idx
        sel = jnp.where(hit, 1.0, sel)
        val = jnp.where(hit, BELOW_ALL, val)
    pen = (sel - 1.0) * (-NEG_INF)
    pen = jnp.concatenate([pen] * NSA_J, axis=1)
    er2 = lax.broadcasted_iota(jnp.int32, (DH - n_slc, cols), 0) + n_slc
    tail = jnp.where(er2 == COL_BLK, slope * SLC_BLOCK, jnp.where(er2 == COL_OFF, slope, 0.0))
    qsa_ref[DH:AUG, :] = jnp.concatenate([pen, tail], axis=0).astype(BF16)

    hw = 2 * LANE
    nhalf = cols // hw

    def slc_update(sc, c, state):
        m_run, l_run, acc = state
        m_new = jnp.maximum(m_run, jnp.max(sc, axis=0, keepdims=True))
        alpha = jnp.exp(m_run - m_new)
        pr = jnp.exp(sc - m_new)
        l_new = alpha * l_run + jnp.sum(pr, axis=0, keepdims=True)
        prb = pr.astype(BF16)
        per = SLC_KCHUNK // kb_rows
        pv = _dot(vst_ref[per * c], prb[0:kb_rows])
        for kb in range(1, per):
            pv = pv + _dot(vst_ref[per * c + kb], prb[kb * kb_rows:(kb + 1) * kb_rows])
        return m_new, l_new, alpha * acc + pv

    def slc_scores(c, h):
        start = pl.multiple_of(c * SLC_KCHUNK, SLC_KCHUNK)
        return _dot(ksa_ref[pl.ds(start, SLC_KCHUNK), :], qsa_ref[:, h * hw:(h + 1) * hw])

    def slc_step(c, carry):
        scs, states = carry
        nxt = tuple(slc_scores(c + 1, h) for h in range(nhalf))
        return nxt, tuple(slc_update(scs[h], c, states[h]) for h in range(nhalf))

    diag = (i * tq) // SLC_KCHUNK
    state0 = (jnp.full((1, hw), NEG_INF, F32), jnp.zeros((1, hw), F32), jnp.zeros((DH, hw), F32))
    init = (tuple(slc_scores(0, h) for h in range(nhalf)), (state0,) * nhalf)
    scs, states = lax.fori_loop(0, diag, slc_step, init)
    krow = diag * SLC_KCHUNK + lax.broadcasted_iota(jnp.int32, (SLC_KCHUNK, 1), 0)
    o_halves = []
    for h in range(nhalf):
        causal = krow <= i * tq + tl[:, h * hw:(h + 1) * hw]
        _, l_fin, acc_fin = slc_update(jnp.where(causal, scs[h], NEG_INF), diag, states[h])
        o_halves.append(acc_fin / l_fin)
    o_slc = jnp.concatenate(o_halves, axis=1)

    wstart = pl.multiple_of(i * tq, tq)
    sw = _dot(kwa_ref[pl.ds(wstart, WIN_SPAN), :], qwa_ref[...]) + bw_ref[...]
    pw = jnp.exp(sw - jnp.max(sw, axis=0, keepdims=True))
    lw = jnp.sum(pw, axis=0, keepdims=True)
    pwb = pw.astype(BF16)
    wblk = i * (tq // kb_rows)
    o_win = _dot(vwt_ref[wblk], pwb[0:kb_rows])
    for kb in range(1, WIN_SPAN // kb_rows):
        o_win = o_win + _dot(vwt_ref[wblk + kb], pwb[kb * kb_rows:(kb + 1) * kb_rows])
    o_win = o_win / lw

    gt = jax.nn.sigmoid(gate_ref[...].astype(F32)).T

    def gate_row(branch):
        return jnp.concatenate([gt[3 * j + branch:3 * j + branch + 1, :] for j in range(NSA_J)], axis=1)

    out_t = gate_row(0) * o_cmp + gate_row(1) * o_slc + gate_row(2) * o_win
    o_ref[...] = jnp.concatenate([out_t[:, j * tq:(j + 1) * tq].T for j in range(NSA_J)], axis=1).astype(o_ref.dtype)


def _nsa(proj, cmp, batch, seq):
    t = proj.shape[0]
    nq = seq // NSA_TQ
    qw = NSA_J * DH
    kv0 = OFF_KV // DH

    def kv_spec(which):
        return pl.BlockSpec((seq, DH), lambda b, g, i: (b, kv0 + 2 * which + g))

    return pl.pallas_call(
        functools.partial(_nsa_body, seq=seq),
        grid=(batch, NSA_GROUPS, nq),
        in_specs=[
            pl.BlockSpec((NSA_TQ, qw), lambda b, g, i: (b * nq + i, g)),
            pl.BlockSpec((NSA_TQ, LANE), lambda b, g, i: (b * nq + i, OFF_NG // LANE + g)),
            pl.BlockSpec((None, None, None, seq // CMP_STRIDE, DH), lambda b, g, i: (b, 0, g, 0, 0)),
            pl.BlockSpec((None, None, None, seq // CMP_STRIDE, DH), lambda b, g, i: (b, 1, g, 0, 0)),
            kv_spec(2), kv_spec(3), kv_spec(4), kv_spec(5),
        ],
        out_specs=pl.BlockSpec((NSA_TQ, qw), lambda b, g, i: (b * nq + i, g)),
        out_shape=jax.ShapeDtypeStruct((t, NSA_Q), BF16),
        scratch_shapes=[
            pltpu.VMEM((seq, AUG), BF16),
            pltpu.VMEM((seq + WINDOW, AUG), BF16),
            pltpu.VMEM((seq // KEY_BLK, DH, KEY_BLK), BF16),
            pltpu.VMEM((seq // KEY_BLK + WIN_PAD_BLOCKS, DH, KEY_BLK), BF16),
            pltpu.VMEM((DH, seq // CMP_STRIDE), BF16),
            pltpu.VMEM((WIN_SPAN, NSA_COLS), F32),
            pltpu.VMEM((AUG, NSA_COLS), BF16),
            pltpu.VMEM((AUG, NSA_COLS), BF16),
        ],
        compiler_params=_params(("parallel", "arbitrary", "arbitrary")),
        name="nsa_attn",
    )(proj, proj, cmp, cmp, proj, proj, proj, proj)


SSD_W = SSD_HG * SSD_P


def _ssd_group(g, conv, a_cum_all, dt_all, z_ref, dskip_ref, nw_ref, state_ref, o_ref, causal, spread):
    lc = CHUNK
    ch = slice(g * SSD_W, (g + 1) * SSD_W)
    xs = conv[:, ch]
    bm = conv[:, SSD_INNER + g * SSD_N:SSD_INNER + (g + 1) * SSD_N]
    cm = conv[:, SSD_INNER + SSD_BC + g * SSD_N:SSD_INNER + SSD_BC + (g + 1) * SSD_N]
    dt = dt_all[:, g * LANE:(g + 1) * LANE]
    a_cum = a_cum_all[:, g * LANE:(g + 1) * LANE]
    a_cum_t = a_cum.T
    dt_x = _dot_exact_rhs(dt, spread)
    acum_x = _dot_exact_rhs(a_cum, spread)
    last_x = acum_x[lc - 1:lc, :]

    xdt = xs * dt_x
    xdt_bf = xdt.astype(BF16)
    cm_bf = cm.astype(BF16)
    cb = _dot_nt(cm_bf, bm.astype(BF16))

    y_heads = []
    for h in range(SSD_HG):
        col = acum_x[:, h * SSD_P:h * SSD_P + 1]
        rowv = a_cum_t[h:h + 1, :]
        decay = jnp.exp(jnp.where(causal, col - rowv, -jnp.inf))
        y_heads.append(_dot((cb * decay).astype(BF16), xdt_bf[:, h * SSD_P:(h + 1) * SSD_P]))
    y_diag = jnp.concatenate(y_heads, axis=-1)

    state = state_ref[:, ch]
    y_off = _dot(cm_bf, state.astype(BF16)) * jnp.exp(acum_x)
    to_end = jnp.exp(last_x - acum_x)
    chunk_state = _dot(bm.T.astype(BF16), (xdt * to_end).astype(BF16))
    state_ref[:, ch] = state * jnp.exp(last_x) + chunk_state

    y = y_diag + y_off + xs * dskip_ref[:, ch]
    y = y * jax.nn.silu(z_ref[:, ch].astype(F32))
    y = y * lax.rsqrt(jnp.mean(y * y, axis=-1, keepdims=True) + EPS)
    o_ref[:, ch] = (y * nw_ref[:, ch]).astype(o_ref.dtype)


def _ssd_body(xbc_ref, z_ref, dt_ref, cw_ref, cb_ref, dtb_ref, alog_ref, dskip_ref, nw_ref, o_ref,
              state_ref, prev_ref, shift_ref):
    first = pl.program_id(1) == 0
    lc = CHUNK
    taps = SSD_CONV - 1

    @pl.when(first)
    def _():
        state_ref[...] = jnp.zeros_like(state_ref)
        prev_ref[...] = jnp.zeros_like(prev_ref)
        r = lax.broadcasted_iota(jnp.int32, (taps * lc, 2 * lc), 0)
        c = lax.broadcasted_iota(jnp.int32, (taps * lc, 2 * lc), 1)
        k = r // lc
        shift_ref[...] = jnp.where(c == lc + (r - k * lc) - (k + 1), 1.0, 0.0).astype(BF16)

    cur = xbc_ref[...]
    shifted = _dot(shift_ref[...], jnp.concatenate([prev_ref[...], cur], axis=0))
    prev_ref[...] = cur
    conv = cb_ref[...] + cw_ref[taps:taps + 1, :] * cur.astype(F32)
    for back in range(1, SSD_CONV):
        conv = conv + cw_ref[taps - back:taps - back + 1, :] * shifted[(back - 1) * lc:back * lc]
    conv = jax.nn.silu(conv)

    dt_all = jax.nn.softplus(dt_ref[...].astype(F32) + dtb_ref[...])
    a_all = dt_all * -jnp.exp(alog_ref[...])
    li = lax.broadcasted_iota(jnp.int32, (lc, lc), 0)
    si = lax.broadcasted_iota(jnp.int32, (lc, lc), 1)
    causal = li >= si
    a_cum_all = _dot_exact_lhs(jnp.where(causal, 1.0, 0.0).astype(BF16), a_all)
    eh = lax.broadcasted_iota(jnp.int32, (LANE, SSD_W), 0)
    ec = lax.broadcasted_iota(jnp.int32, (LANE, SSD_W), 1) // SSD_P
    spread = jnp.where(eh == ec, 1.0, 0.0).astype(BF16)
    for g in range(SSD_GROUPS):
        _ssd_group(g, conv, a_cum_all, dt_all, z_ref, dskip_ref, nw_ref, state_ref, o_ref, causal, spread)


def _ssd(proj, lw, layer, batch, seq):
    t = proj.shape[0]
    nc = seq // CHUNK
    glanes = SSD_GROUPS * LANE

    def rows(b, c):
        return pl.multiple_of((b * nc + c) * CHUNK, CHUNK)

    def window(width, offset):
        return pl.BlockSpec((pl.Element(CHUNK), pl.Element(width)), lambda b, c: (rows(b, c), offset))

    return pl.pallas_call(
        _ssd_body,
        grid=(batch, nc),
        in_specs=[
            window(SSD_CONV_DIM, OFF_XBC),
            window(SSD_INNER, OFF_Z),
            window(glanes, OFF_DT),
            pl.BlockSpec((None, SSD_CONV, SSD_CONV_DIM), lambda b, c: (layer, 0, 0)),
            pl.BlockSpec((None, 1, SSD_CONV_DIM), lambda b, c: (layer, 0, 0)),
            pl.BlockSpec((None, 1, glanes), lambda b, c: (layer, 0, 0)),
            pl.BlockSpec((None, 1, glanes), lambda b, c: (layer, 0, 0)),
            pl.BlockSpec((None, 1, SSD_INNER), lambda b, c: (layer, 0, 0)),
            pl.BlockSpec((None, 1, SSD_INNER), lambda b, c: (layer, 0, 0)),
        ],
        out_specs=pl.BlockSpec((CHUNK, SSD_INNER), lambda b, c: (b * nc + c, 0)),
        out_shape=jax.ShapeDtypeStruct((t, SSD_INNER), BF16),
        scratch_shapes=[
            pltpu.VMEM((SSD_N, SSD_INNER), F32),
            pltpu.VMEM((CHUNK, SSD_CONV_DIM), BF16),
            pltpu.VMEM(((SSD_CONV - 1) * CHUNK, 2 * CHUNK), BF16),
        ],
        compiler_params=_params(("parallel", "arbitrary")),
        name="ssd",
    )(proj, proj, proj, lw["conv_w"], lw["conv_b"], lw["dt_bias"], lw["a_log"], lw["d_skip"], lw["ssd_norm"])


def _ret_log_gamma(h):
    return float(np.log1p(-np.exp2(-5.0 - h)))


def _ret_body(q_ref, k_ref, v0_ref, v1_ref, g0_ref, g1_ref, nw_ref, o_ref, state_ref):
    lc = CHUNK
    half = RET_HEADS // 2
    v_refs = (v0_ref, v1_ref)
    g_refs = (g0_ref, g1_ref)

    @pl.when(pl.program_id(1) == 0)
    def _():
        state_ref[...] = jnp.zeros_like(state_ref)

    li = lax.broadcasted_iota(jnp.int32, (lc, lc), 0)
    si = lax.broadcasted_iota(jnp.int32, (lc, lc), 1)
    diff = (li - si).astype(F32)
    idx = lax.broadcasted_iota(jnp.int32, (lc, 1), 0).astype(F32)

    for h in range(RET_HEADS):
        log_g = _ret_log_gamma(h)
        dmat = jnp.where(diff >= 0, jnp.exp(diff * log_g), 0.0)
        k_dec = jnp.exp((lc - 1.0 - idx) * log_g)
        q_dec = jnp.exp((idx + 1.0) * log_g)
        q = q_ref[:, h * RET_DK:(h + 1) * RET_DK].astype(F32) * (RET_DK ** -0.5)
        k = k_ref[:, h * RET_DK:(h + 1) * RET_DK].astype(F32)
        vsl = slice((h % half) * RET_DV, (h % half + 1) * RET_DV)
        v = v_refs[h // half][:, vsl]
        scores = _dot_nt(q.astype(BF16), k.astype(BF16)) * dmat
        state = state_ref[h]
        y = _dot(scores.astype(BF16), v) + _dot((q * q_dec).astype(BF16), state.astype(BF16))
        chunk_kv = _dot((k * k_dec).T.astype(BF16), v)
        state_ref[h] = state * float(np.exp(lc * log_g)) + chunk_kv

        mu = jnp.mean(y, axis=-1, keepdims=True)
        yc = y - mu
        var = jnp.mean(yc * yc, axis=-1, keepdims=True)
        y = yc * lax.rsqrt(var + EPS) * nw_ref[h]
        gate = jax.nn.silu(g_refs[h // half][:, vsl].astype(F32))
        o_ref[:, h * RET_DV:(h + 1) * RET_DV] = (gate * y).astype(o_ref.dtype)


def _retention(proj, nw, layer, batch, seq):
    t = proj.shape[0]
    nc = seq // CHUNK
    hw = RET_V // 2
    q0 = OFF_RQK // RET_QK
    v0 = OFF_RV // hw
    g0 = OFF_RG // hw
    return pl.pallas_call(
        _ret_body,
        grid=(batch, nc),
        in_specs=[
            pl.BlockSpec((CHUNK, RET_QK), lambda b, c: (b * nc + c, q0)),
            pl.BlockSpec((CHUNK, RET_QK), lambda b, c: (b * nc + c, q0 + 1)),
            pl.BlockSpec((CHUNK, hw), lambda b, c: (b * nc + c, v0)),
            pl.BlockSpec((CHUNK, hw), lambda b, c: (b * nc + c, v0 + 1)),
            pl.BlockSpec((CHUNK, hw), lambda b, c: (b * nc + c, g0)),
            pl.BlockSpec((CHUNK, hw), lambda b, c: (b * nc + c, g0 + 1)),
            pl.BlockSpec((None, RET_HEADS, 1, RET_DV), lambda b, c: (layer, 0, 0, 0)),
        ],
        out_specs=pl.BlockSpec((CHUNK, RET_V), lambda b, c: (b * nc + c, 0)),
        out_shape=jax.ShapeDtypeStruct((t, RET_V), BF16),
        scratch_shapes=[pltpu.VMEM((RET_HEADS, RET_DK, RET_DV), F32)],
        compiler_params=_params(("parallel", "arbitrary")),
        name="retention",
    )(proj, proj, proj, proj, proj, proj, nw)


def _per_group_rows(seg, groups):
    depth, rows, d = seg.shape
    per = rows // groups
    seg = seg.reshape(depth, groups, per, d)
    seg = jnp.pad(seg, [(0, 0), (0, 0), (0, LANE - per), (0, 0)])
    return seg.reshape(depth, groups * LANE, d)


def _small_w_t(w_t):
    o_ng = IN_SPLITS[0] + IN_SPLITS[1]
    o_dt = o_ng + IN_SPLITS[2] + IN_SPLITS[3] + IN_SPLITS[4]
    ng = _per_group_rows(w_t[:, o_ng:o_ng + IN_SPLITS[2], :], NSA_GROUPS)
    dt = _per_group_rows(w_t[:, o_dt:o_dt + IN_SPLITS[5], :], SSD_GROUPS)
    pad = jnp.zeros((w_t.shape[0], N_SMALL - ng.shape[1] - dt.shape[1], w_t.shape[2]), w_t.dtype)
    return jnp.concatenate([ng, dt, pad], axis=1).astype(BF16)


def _per_group_lanes(v, groups):
    depth, heads = v.shape
    per = heads // groups
    v = jnp.pad(v.reshape(depth, groups, per), [(0, 0), (0, 0), (0, LANE - per)])
    return v.reshape(depth, 1, groups * LANE)


def _layer(x, lw, layer, batch, seq):
    u = _rms(x, lw["norm_mix"], layer, BF16)
    proj = _inproj(u, lw["w_t"], lw["w_small_t"], layer)

    kv = proj[:, OFF_KV:OFF_KV + 2 * NSA_KV].reshape(batch, seq // CMP_STRIDE, CMP_STRIDE, 2, NSA_GROUPS, DH)
    xc = kv.transpose(0, 3, 4, 1, 2, 5).reshape(batch, 2, NSA_GROUPS, seq // CMP_STRIDE, CMP_STRIDE * DH)
    cmp = _compress(xc, lw["cmp_pe"][layer], lw["cmp_w1"][layer], lw["cmp_w2"][layer])

    y_nsa = _nsa(proj, cmp, batch, seq)
    y_ssd = _ssd(proj, lw, layer, batch, seq)
    y_ret = _retention(proj, lw["ret_norm"], layer, batch, seq)

    merged = _merge(y_nsa, y_ssd, y_ret, proj, lw["p_nsa"], lw["p_ssd"], lw["p_ret"], layer)
    x = _mm_res(merged, lw["w_out"], x, layer, 512)
    f = _rms(x, lw["norm_ffn"], layer, BF16)
    hidden = _ffn_up(f, lw["w_gate"], lw["w_up"], layer)
    return _mm_res(hidden, lw["w_down"], x, layer, 256)


def kernel(x, norm_mix, w_in, cmp_k_pe, cmp_k_w1, cmp_k_w2, cmp_v_pe, cmp_v_w1, cmp_v_w2, conv_w, conv_b, dt_bias, a_log, d_skip, ssd_norm, ret_norm, p_nsa, p_ssd, p_ret, w_out, norm_ffn, w_gate, w_up, w_down, norm_final):
    batch, seq, d = x.shape
    depth = w_in.shape[0]
    w_t = jnp.swapaxes(w_in, 1, 2)
    pe = jnp.stack([cmp_k_pe, cmp_v_pe], axis=1).reshape(depth, 2, 1, CMP_LEN * DH)
    lw = {
        "norm_mix": norm_mix[:, None, :],
        "w_t": w_t,
        "w_small_t": _small_w_t(w_t),
        "cmp_pe": jnp.pad(pe, [(0, 0), (0, 0), (0, 7), (0, 0)]).astype(BF16),
        "cmp_w1": jnp.stack([cmp_k_w1, cmp_v_w1], axis=1).astype(BF16),
        "cmp_w2": jnp.stack([cmp_k_w2, cmp_v_w2], axis=1).astype(BF16),
        "conv_w": conv_w,
        "conv_b": conv_b[:, None, :],
        "dt_bias": _per_group_lanes(dt_bias, SSD_GROUPS),
        "a_log": _per_group_lanes(a_log, SSD_GROUPS),
        "d_skip": jnp.repeat(d_skip, SSD_P, axis=-1)[:, None, :],
        "ssd_norm": ssd_norm[:, None, :],
        "ret_norm": ret_norm[:, :, None, :],
        "p_nsa": p_nsa, "p_ssd": p_ssd, "p_ret": p_ret, "w_out": w_out,
        "norm_ffn": norm_ffn[:, None, :],
        "w_gate": w_gate, "w_up": w_up, "w_down": w_down,
    }
    h = x.reshape(batch * seq, d)
    for layer in range(depth):
        h = _layer(h, lw, layer, batch, seq)
    return _rms(h, norm_final[None, None, :], 0, F32).reshape(batch, seq, d)
```

```python
import functools

import jax
import jax.numpy as jnp
import numpy as np
from jax import lax
from jax.experimental import pallas as pl
from jax.experimental.pallas import tpu as pltpu

F32 = jnp.float32
BF16 = jnp.bfloat16

D_MODEL = 2048
NSA_HEADS = 8
NSA_GROUPS = 2
NSA_J = NSA_HEADS // NSA_GROUPS
DH = 128
CMP_LEN = 32
CMP_STRIDE = 16
SLC_BLOCK = 64
SLC_TOPK = 8
WINDOW = 512
SSD_INNER = D_MODEL
SSD_P = 64
SSD_HEADS = SSD_INNER // SSD_P
SSD_GROUPS = 4
SSD_HG = SSD_HEADS // SSD_GROUPS
SSD_N = 128
SSD_CONV = 4
CHUNK = 128
RET_HEADS = 4
RET_DK = 128
RET_DV = 256
D_FF = -(-8 * D_MODEL // (3 * 256)) * 256
NSA_Q = NSA_HEADS * DH
NSA_KV = NSA_GROUPS * DH
SSD_BC = SSD_GROUPS * SSD_N
SSD_CONV_DIM = SSD_INNER + 2 * SSD_BC
RET_QK = RET_HEADS * RET_DK
RET_V = RET_HEADS * RET_DV
IN_SPLITS = (NSA_Q, 6 * NSA_KV, 3 * NSA_HEADS, SSD_INNER, SSD_CONV_DIM, SSD_HEADS,
             2 * RET_QK, RET_V, RET_V, 3 * D_MODEL)
D_IN = sum(IN_SPLITS)
NEG_INF = -1e30
FORCE_SCORE = 1e9
BELOW_ALL = -3e38
EPS = 1e-6

LANE = 128

OFF_Q = 0
OFF_KV = OFF_Q + NSA_Q
OFF_Z = OFF_KV + 6 * NSA_KV
OFF_XBC = OFF_Z + SSD_INNER
OFF_RQK = OFF_XBC + SSD_CONV_DIM
OFF_RV = OFF_RQK + 2 * RET_QK
OFF_RG = OFF_RV + RET_V
OFF_MG = OFF_RG + RET_V
OFF_NG = OFF_MG + 3 * D_MODEL
OFF_DT = OFF_NG + NSA_GROUPS * LANE
N_USED = OFF_DT + SSD_GROUPS * LANE
PROJ_TN = 512
N_PACK = -(-N_USED // PROJ_TN) * PROJ_TN
N_MAIN_BLOCKS = OFF_NG // PROJ_TN
N_SMALL = N_PACK - OFF_NG

SHIFT_B = IN_SPLITS[2]
SHIFT_C = IN_SPLITS[2] + IN_SPLITS[5]
BLK_B = OFF_Z // PROJ_TN
BLK_C = OFF_RQK // PROJ_TN

ROW_TILE = 2048
VMEM_LIMIT = 56 * 1024 * 1024


def _dot(a, b):
    return jnp.dot(a, b, preferred_element_type=F32)


def _dot_nt(a, b):
    return lax.dot_general(a, b, (((1,), (1,)), ((), ())), preferred_element_type=F32)


def _split3(x):
    hi = x.astype(BF16)
    r1 = x - hi.astype(F32)
    mid = r1.astype(BF16)
    lo = (r1 - mid.astype(F32)).astype(BF16)
    return hi, mid, lo


def _dot_exact_rhs(x, m_bf16):
    hi, mid, lo = _split3(x)
    return _dot(hi, m_bf16) + _dot(mid, m_bf16) + _dot(lo, m_bf16)


def _dot_exact_lhs(m_bf16, x):
    hi, mid, lo = _split3(x)
    return _dot(m_bf16, hi) + _dot(m_bf16, mid) + _dot(m_bf16, lo)


def _params(sem):
    return pltpu.CompilerParams(dimension_semantics=sem, vmem_limit_bytes=VMEM_LIMIT)


def _rms_body(x_ref, nw_ref, o_ref):
    slab = 128

    def body(r, carry):
        sl = pl.ds(pl.multiple_of(r * slab, slab), slab)
        x = x_ref[sl, :]
        ms = jnp.mean(x * x, axis=-1, keepdims=True)
        o_ref[sl, :] = (x * lax.rsqrt(ms + EPS) * nw_ref[...]).astype(o_ref.dtype)
        return carry

    lax.fori_loop(0, x_ref.shape[0] // slab, body, 0)


def _rms(x, nw, layer, out_dtype):
    t, d = x.shape
    tm = 1024
    return pl.pallas_call(
        _rms_body,
        grid=(t // tm,),
        in_specs=[pl.BlockSpec((tm, d), lambda i: (i, 0)),
                  pl.BlockSpec((None, 1, d), lambda i: (layer, 0, 0))],
        out_specs=pl.BlockSpec((tm, d), lambda i: (i, 0)),
        out_shape=jax.ShapeDtypeStruct((t, d), out_dtype),
        compiler_params=_params(("parallel",)),
        name="rmsnorm",
    )(x, nw)


def _inproj_body(u_ref, w_ref, ws_ref, o_ref):
    j = pl.program_id(1)

    @pl.when(j < N_MAIN_BLOCKS)
    def _():
        o_ref[...] = _dot_nt(u_ref[...], w_ref[0].astype(BF16)).astype(o_ref.dtype)

    @pl.when(j >= N_MAIN_BLOCKS)
    def _():
        o_ref[...] = _dot_nt(u_ref[...], ws_ref[...].astype(BF16)).astype(o_ref.dtype)


def _inproj(u, w_t, w_small_t, layer):
    t, d = u.shape
    tm, tn = ROW_TILE, PROJ_TN
    last_main = N_MAIN_BLOCKS - 1

    def w_index(i, j):
        jc = jnp.minimum(j, last_main)
        shift = jnp.where(jc >= BLK_C, SHIFT_C, jnp.where(jc >= BLK_B, SHIFT_B, 0))
        return (layer, pl.multiple_of(jc * tn + shift, 8), 0)

    return pl.pallas_call(
        _inproj_body,
        grid=(t // tm, N_PACK // tn),
        in_specs=[
            pl.BlockSpec((tm, d), lambda i, j: (i, 0)),
            pl.BlockSpec((pl.Element(1), pl.Element(tn), pl.Element(d)), w_index),
            pl.BlockSpec((None, tn, d), lambda i, j: (layer, jnp.maximum(j - N_MAIN_BLOCKS, 0), 0)),
        ],
        out_specs=pl.BlockSpec((tm, tn), lambda i, j: (i, j)),
        out_shape=jax.ShapeDtypeStruct((t, N_PACK), BF16),
        compiler_params=_params(("parallel", "arbitrary")),
        name="inproj",
    )(u, w_t, w_small_t)


def _merge_body(yn_ref, ys_ref, yr_ref, gn_ref, gs_ref, gr_ref, pn_ref, ps_ref, pr_ref, o_ref):
    def gate(g_ref):
        return jax.nn.sigmoid(g_ref[...].astype(F32))

    acc = gate(gn_ref) * _dot(yn_ref[...], pn_ref[...].astype(BF16))
    acc = acc + gate(gs_ref) * _dot(ys_ref[...], ps_ref[...].astype(BF16))
    acc = acc + gate(gr_ref) * _dot(yr_ref[...], pr_ref[...].astype(BF16))
    o_ref[...] = acc.astype(o_ref.dtype)


def _merge(y_nsa, y_ssd, y_ret, proj, p_nsa, p_ssd, p_ret, layer):
    t = y_nsa.shape[0]
    tm, tn = 1024, 512
    g0 = OFF_MG // tn
    gstep = D_MODEL // tn
    return pl.pallas_call(
        _merge_body,
        grid=(t // tm, D_MODEL // tn),
        in_specs=[
            pl.BlockSpec((tm, NSA_Q), lambda i, j: (i, 0)),
            pl.BlockSpec((tm, SSD_INNER), lambda i, j: (i, 0)),
            pl.BlockSpec((tm, RET_V), lambda i, j: (i, 0)),
            pl.BlockSpec((tm, tn), lambda i, j: (i, g0 + j)),
            pl.BlockSpec((tm, tn), lambda i, j: (i, g0 + gstep + j)),
            pl.BlockSpec((tm, tn), lambda i, j: (i, g0 + 2 * gstep + j)),
            pl.BlockSpec((None, NSA_Q, tn), lambda i, j: (layer, 0, j)),
            pl.BlockSpec((None, SSD_INNER, tn), lambda i, j: (layer, 0, j)),
            pl.BlockSpec((None, RET_V, tn), lambda i, j: (layer, 0, j)),
        ],
        out_specs=pl.BlockSpec((tm, tn), lambda i, j: (i, j)),
        out_shape=jax.ShapeDtypeStruct((t, D_MODEL), BF16),
        compiler_params=_params(("parallel", "arbitrary")),
        name="merge",
    )(y_nsa, y_ssd, y_ret, proj, proj, proj, p_nsa, p_ssd, p_ret)


def _mm_res_body(a_ref, w_ref, r_ref, o_ref):
    o_ref[...] = r_ref[...] + _dot(a_ref[...], w_ref[...].astype(BF16))


def _mm_res(a, w, res, layer, tn):
    t, k = a.shape
    n = w.shape[2]
    tm = ROW_TILE
    a_mode = pl.Buffered(1) if k > D_MODEL else None
    return pl.pallas_call(
        _mm_res_body,
        grid=(t // tm, n // tn),
        in_specs=[
            pl.BlockSpec((tm, k), lambda i, j: (i, 0), pipeline_mode=a_mode),
            pl.BlockSpec((None, k, tn), lambda i, j: (layer, 0, j)),
            pl.BlockSpec((tm, tn), lambda i, j: (i, j)),
        ],
        out_specs=pl.BlockSpec((tm, tn), lambda i, j: (i, j)),
        out_shape=jax.ShapeDtypeStruct((t, n), F32),
        compiler_params=_params(("parallel", "arbitrary")),
        name="mm_res",
    )(a, w, res)


def _ffn_up_body(f_ref, wg_ref, wu_ref, o_ref):
    f = f_ref[...]
    gate = _dot(f, wg_ref[...].astype(BF16))
    up = _dot(f, wu_ref[...].astype(BF16))
    o_ref[...] = (jax.nn.silu(gate) * up).astype(o_ref.dtype)


def _ffn_up(f, wg, wu, layer):
    t, d = f.shape
    n = wg.shape[2]
    tm, tn = ROW_TILE, 512
    return pl.pallas_call(
        _ffn_up_body,
        grid=(t // tm, n // tn),
        in_specs=[
            pl.BlockSpec((tm, d), lambda i, j: (i, 0)),
            pl.BlockSpec((None, d, tn), lambda i, j: (layer, 0, j)),
            pl.BlockSpec((None, d, tn), lambda i, j: (layer, 0, j)),
        ],
        out_specs=pl.BlockSpec((tm, tn), lambda i, j: (i, j)),
        out_shape=jax.ShapeDtypeStruct((t, n), BF16),
        compiler_params=_params(("parallel", "arbitrary")),
        name="ffn_up",
    )(f, wg, wu)


def _compress_body(x_ref, pe_ref, w1_ref, w2_ref, o_ref):
    half = (CMP_LEN // 2) * DH
    x = x_ref[...]
    a = _dot(x, w1_ref[:half, :])
    b = _dot(x, w1_ref[half:, :])
    bias = _dot(pe_ref[...], w1_ref[...])[0:1, :]
    nblk = x.shape[0]
    h = a + pltpu.roll(b, nblk - 1, 0) + bias
    h = jax.nn.gelu(h)
    o_ref[...] = _dot(h.astype(BF16), w2_ref[...]).astype(o_ref.dtype)


def _compress(xc, pe, w1, w2):
    b, _, g, nchunk, width = xc.shape
    return pl.pallas_call(
        _compress_body,
        grid=(b, 2, g),
        in_specs=[
            pl.BlockSpec((None, None, None, nchunk, width), lambda bi, m, gi: (bi, m, gi, 0, 0)),
            pl.BlockSpec((None, 8, CMP_LEN * DH), lambda bi, m, gi: (m, 0, 0)),
            pl.BlockSpec((None, CMP_LEN * DH, DH), lambda bi, m, gi: (m, 0, 0)),
            pl.BlockSpec((None, DH, DH), lambda bi, m, gi: (m, 0, 0)),
        ],
        out_specs=pl.BlockSpec((None, None, None, nchunk, DH), lambda bi, m, gi: (bi, m, gi, 0, 0)),
        out_shape=jax.ShapeDtypeStruct((b, 2, g, nchunk, DH), BF16),
        compiler_params=_params(("parallel", "arbitrary", "arbitrary")),
        name="nsa_compress",
    )(xc, pe, w1, w2)


NSA_TQ = 256
KEY_BLK = 256
VT_ROWS = DH + 16
NSA_COLS = NSA_J * NSA_TQ
SLC_KCHUNK = 256
WIN_SPAN = WINDOW + NSA_TQ
WIN_PAD_BLOCKS = WINDOW // KEY_BLK
AUG = 2 * DH
COL_BLK = SLC_BLOCK // 2
COL_OFF = COL_BLK + 1


def _nsa_body(q_ref, gate_ref, kc_ref, vc_ref, ks_ref, vs_ref, kw_ref, vw_ref, o_ref,
              ksa_ref, kwa_ref, vst_ref, vwt_ref, vct_ref, bw_ref, qsa_ref, qwa_ref, *, seq):
    g = pl.program_id(1)
    i = pl.program_id(2)
    tq = NSA_TQ
    cols = NSA_COLS
    n_slc = seq // SLC_BLOCK
    kb_rows = KEY_BLK
    nkb = seq // kb_rows

    cc = lax.broadcasted_iota(jnp.int32, (1, cols), 1)
    head = cc // tq
    tl = cc - head * tq
    slope_g = jnp.where(g == 0, 0.5, 0.5 ** (NSA_J + 1)).astype(F32)
    slope = slope_g * jnp.where(head == 0, 1.0, jnp.where(head == 1, 0.5, jnp.where(head == 2, 0.25, 0.125)))
    tl_f = tl.astype(F32)

    @pl.when(i == 0)
    def _init():
        def blk_body(kb, carry):
            r0 = pl.multiple_of(kb * kb_rows, kb_rows)
            rows = pl.ds(r0, kb_rows)
            key = r0 + lax.broadcasted_iota(jnp.int32, (kb_rows, DH), 0)
            col = lax.broadcasted_iota(jnp.int32, (kb_rows, DH), 1)
            blk = key // SLC_BLOCK
            off = key - blk * SLC_BLOCK
            extra = jnp.where(col == blk, 1.0, 0.0)
            extra = jnp.where(col == COL_BLK, blk.astype(F32), extra)
            extra = jnp.where(col == COL_OFF, off.astype(F32), extra)
            ksa_ref[rows, 0:DH] = ks_ref[rows, :]
            ksa_ref[rows, DH:AUG] = extra.astype(BF16)
            prow = pl.ds(pl.multiple_of(r0 + WINDOW, kb_rows), kb_rows)
            kwa_ref[prow, 0:DH] = kw_ref[rows, :]
            kwa_ref[prow, DH:AUG] = jnp.zeros((kb_rows, DH), BF16)
            orow = lax.broadcasted_iota(jnp.int32, (VT_ROWS - DH, kb_rows), 0)
            ones_row = jnp.where(orow == 0, 1.0, 0.0).astype(BF16)
            vst_ref[kb, 0:DH, :] = vs_ref[rows, :].astype(F32).T.astype(BF16)
            vst_ref[kb, DH:VT_ROWS, :] = ones_row
            vwt_ref[kb + WIN_PAD_BLOCKS, 0:DH, :] = vw_ref[rows, :].astype(F32).T.astype(BF16)
            vwt_ref[kb + WIN_PAD_BLOCKS, DH:VT_ROWS, :] = ones_row
            return carry

        lax.fori_loop(0, nkb, blk_body, 0)
        pcol = lax.broadcasted_iota(jnp.int32, (WINDOW, DH), 1)
        kwa_ref[0:WINDOW, 0:DH] = jnp.zeros((WINDOW, DH), BF16)
        kwa_ref[0:WINDOW, DH:AUG] = jnp.where(pcol == 0, 1.0, 0.0).astype(BF16)
        vwt_ref[0:WIN_PAD_BLOCKS] = jnp.zeros((WIN_PAD_BLOCKS, VT_ROWS, kb_rows), BF16)
        vct_ref[...] = vc_ref[...].astype(F32).T.astype(BF16)
        wk = lax.broadcasted_iota(jnp.int32, (WIN_SPAN, cols), 0)
        d = (WINDOW + tl - wk).astype(F32)
        bw_ref[...] = jnp.where((d >= 0) & (d < WINDOW), -slope * d, NEG_INF)

    q = q_ref[...].astype(F32) * (DH ** -0.5)
    qt = jnp.concatenate([q[:, j * DH:(j + 1) * DH].T for j in range(NSA_J)], axis=1).astype(BF16)
    qsa_ref[0:DH, :] = qt
    qwa_ref[0:DH, :] = qt
    er = lax.broadcasted_iota(jnp.int32, (DH, cols), 0)
    qwa_ref[DH:AUG, :] = jnp.where(er == 0, NEG_INF, 0.0).astype(BF16)

    ncmp = kc_ref.shape[0]
    n_idx = lax.broadcasted_iota(jnp.int32, (ncmp, 1), 0)
    rel_end = (n_idx * CMP_STRIDE + (CMP_LEN - 1)).astype(F32) - tl_f
    base = (i * tq).astype(F32)
    s = _dot(kc_ref[...], qt) + slope * rel_end
    s = jnp.where(rel_end <= base, s, NEG_INF)
    p = jnp.exp(s - jnp.max(s, axis=0, keepdims=True))
    p = p / jnp.sum(p, axis=0, keepdims=True)
    p = p * ((base + tl_f) >= (CMP_LEN - 1)).astype(F32)
    o_cmp = _dot(vct_ref[...], p.astype(BF16))

    psum = p[:, 0:tq]
    for j in range(1, NSA_J):
        psum = psum + p[:, j * tq:(j + 1) * tq]
    kk = lax.broadcasted_iota(jnp.int32, (LANE, ncmp), 0)
    nn = lax.broadcasted_iota(jnp.int32, (LANE, ncmp), 1)
    overlap = ((nn * CMP_STRIDE < (kk + 1) * SLC_BLOCK)
               & (nn * CMP_STRIDE + (CMP_LEN - 1) >= kk * SLC_BLOCK)
               & (kk < n_slc))
    imp = _dot_exact_lhs(jnp.where(overlap, 1.0, 0.0).astype(BF16), psum)[0:n_slc, :]
    blk = lax.broadcasted_iota(jnp.int32, (n_slc, tq), 0)
    blk_f = blk.astype(F32)
    cur = (i * tq + lax.broadcasted_iota(jnp.int32, (1, tq), 1)) // SLC_BLOCK
    val = jnp.where(blk > cur, NEG_INF, imp)
    val = jnp.where((blk == cur) | (blk == 0), FORCE_SCORE, val)
    sel = jnp.zeros((n_slc, tq), F32)
    for _ in range(min(SLC_TOPK, n_slc)):
        mx = jnp.max(val, axis=0, keepdims=True)
        idx = jnp.min(jnp.where(val == mx, blk_f, float(n_slc)), axis=0, keepdims=True)
        hit = blk_f == idx
        sel = jnp.where(hit, 1.0, sel)
        val = jnp.where(hit, BELOW_ALL, val)
    pen = (sel - 1.0) * (-NEG_INF)
    pen = jnp.concatenate([pen] * NSA_J, axis=1)
    er2 = lax.broadcasted_iota(jnp.int32, (DH - n_slc, cols), 0) + n_slc
    tail = jnp.where(er2 == COL_BLK, slope * SLC_BLOCK, jnp.where(er2 == COL_OFF, slope, 0.0))
    qsa_ref[DH:AUG, :] = jnp.concatenate([pen, tail], axis=0).astype(BF16)

    hw = 2 * LANE
    nhalf = cols // hw

    def slc_update(sc, c, state):
        m_run, acc = state
        m_new = jnp.maximum(m_run, jnp.max(sc, axis=0, keepdims=True))
        alpha = jnp.exp(m_run - m_new)
        prb = jnp.exp(sc - m_new).astype(BF16)
        return m_new, alpha * acc + _dot(vst_ref[c], prb)

    def slc_scores(c, h):
        start = pl.multiple_of(c * SLC_KCHUNK, SLC_KCHUNK)
        return _dot(ksa_ref[pl.ds(start, SLC_KCHUNK), :], qsa_ref[:, h * hw:(h + 1) * hw])

    def slc_step(c, carry):
        scs, states = carry
        nxt = tuple(slc_scores(c + 1, h) for h in range(nhalf))
        return nxt, tuple(slc_update(scs[h], c, states[h]) for h in range(nhalf))

    diag = (i * tq) // SLC_KCHUNK
    state0 = (jnp.full((1, hw), NEG_INF, F32), jnp.zeros((VT_ROWS, hw), F32))
    init = (tuple(slc_scores(0, h) for h in range(nhalf)), (state0,) * nhalf)
    scs, states = lax.fori_loop(0, diag, slc_step, init)
    krow = diag * SLC_KCHUNK + lax.broadcasted_iota(jnp.int32, (SLC_KCHUNK, 1), 0)
    o_halves = []
    for h in range(nhalf):
        causal = krow <= i * tq + tl[:, h * hw:(h + 1) * hw]
        _, acc_fin = slc_update(jnp.where(causal, scs[h], NEG_INF), diag, states[h])
        o_halves.append(acc_fin[0:DH] / acc_fin[DH:DH + 1])
    o_slc = jnp.concatenate(o_halves, axis=1)

    wstart = pl.multiple_of(i * tq, tq)
    sw = _dot(kwa_ref[pl.ds(wstart, WIN_SPAN), :], qwa_ref[...]) + bw_ref[...]
    pwb = jnp.exp(sw - jnp.max(sw, axis=0, keepdims=True)).astype(BF16)
    wblk = i * (tq // kb_rows)
    vwin = jnp.concatenate([vwt_ref[wblk + kb] for kb in range(WIN_SPAN // kb_rows)], axis=1)
    o_aug = _dot(vwin, pwb)
    o_win = o_aug[0:DH] / o_aug[DH:DH + 1]

    gt = jax.nn.sigmoid(gate_ref[...].astype(F32)).T

    def gate_row(branch):
        return jnp.concatenate([gt[3 * j + branch:3 * j + branch + 1, :] for j in range(NSA_J)], axis=1)

    out_t = gate_row(0) * o_cmp + gate_row(1) * o_slc + gate_row(2) * o_win
    o_ref[...] = jnp.concatenate([out_t[:, j * tq:(j + 1) * tq].T for j in range(NSA_J)], axis=1).astype(o_ref.dtype)


def _nsa(proj, cmp, batch, seq):
    t = proj.shape[0]
    nq = seq // NSA_TQ
    qw = NSA_J * DH
    kv0 = OFF_KV // DH

    def kv_spec(which):
        return pl.BlockSpec((seq, DH), lambda b, g, i: (b, kv0 + 2 * which + g))

    return pl.pallas_call(
        functools.partial(_nsa_body, seq=seq),
        grid=(batch, NSA_GROUPS, nq),
        in_specs=[
            pl.BlockSpec((NSA_TQ, qw), lambda b, g, i: (b * nq + i, g)),
            pl.BlockSpec((NSA_TQ, LANE), lambda b, g, i: (b * nq + i, OFF_NG // LANE + g)),
            pl.BlockSpec((None, None, None, seq // CMP_STRIDE, DH), lambda b, g, i: (b, 0, g, 0, 0)),
            pl.BlockSpec((None, None, None, seq // CMP_STRIDE, DH), lambda b, g, i: (b, 1, g, 0, 0)),
            kv_spec(2), kv_spec(3), kv_spec(4), kv_spec(5),
        ],
        out_specs=pl.BlockSpec((NSA_TQ, qw), lambda b, g, i: (b * nq + i, g)),
        out_shape=jax.ShapeDtypeStruct((t, NSA_Q), BF16),
        scratch_shapes=[
            pltpu.VMEM((seq, AUG), BF16),
            pltpu.VMEM((seq + WINDOW, AUG), BF16),
            pltpu.VMEM((seq // KEY_BLK, VT_ROWS, KEY_BLK), BF16),
            pltpu.VMEM((seq // KEY_BLK + WIN_PAD_BLOCKS, VT_ROWS, KEY_BLK), BF16),
            pltpu.VMEM((DH, seq // CMP_STRIDE), BF16),
            pltpu.VMEM((WIN_SPAN, NSA_COLS), F32),
            pltpu.VMEM((AUG, NSA_COLS), BF16),
            pltpu.VMEM((AUG, NSA_COLS), BF16),
        ],
        compiler_params=_params(("parallel", "arbitrary", "arbitrary")),
        name="nsa_attn",
    )(proj, proj, cmp, cmp, proj, proj, proj, proj)


SSD_W = SSD_HG * SSD_P


def _ssd_group(g, conv, a_cum_all, dt_all, z_ref, dskip_ref, nw_ref, state_ref, o_ref, causal, spread):
    lc = CHUNK
    ch = slice(g * SSD_W, (g + 1) * SSD_W)
    xs = conv[:, ch]
    bm = conv[:, SSD_INNER + g * SSD_N:SSD_INNER + (g + 1) * SSD_N]
    cm = conv[:, SSD_INNER + SSD_BC + g * SSD_N:SSD_INNER + SSD_BC + (g + 1) * SSD_N]
    dt = dt_all[:, g * LANE:(g + 1) * LANE]
    a_cum = a_cum_all[:, g * LANE:(g + 1) * LANE]
    a_cum_t = a_cum.T
    dt_x = _dot_exact_rhs(dt, spread)
    acum_x = _dot_exact_rhs(a_cum, spread)
    last_x = acum_x[lc - 1:lc, :]

    xdt = xs * dt_x
    xdt_bf = xdt.astype(BF16)
    cm_bf = cm.astype(BF16)
    cb = _dot_nt(cm_bf, bm.astype(BF16))

    y_heads = []
    for h in range(SSD_HG):
        col = acum_x[:, h * SSD_P:h * SSD_P + 1]
        rowv = a_cum_t[h:h + 1, :]
        decay = jnp.exp(jnp.where(causal, col - rowv, -jnp.inf))
        y_heads.append(_dot((cb * decay).astype(BF16), xdt_bf[:, h * SSD_P:(h + 1) * SSD_P]))
    y_diag = jnp.concatenate(y_heads, axis=-1)

    state = state_ref[:, ch]
    y_off = _dot(cm_bf, state.astype(BF16)) * jnp.exp(acum_x)
    to_end = jnp.exp(last_x - acum_x)
    chunk_state = _dot(bm.T.astype(BF16), (xdt * to_end).astype(BF16))
    state_ref[:, ch] = state * jnp.exp(last_x) + chunk_state

    y = y_diag + y_off + xs * dskip_ref[:, ch]
    y = y * jax.nn.silu(z_ref[:, ch].astype(F32))
    y = y * lax.rsqrt(jnp.mean(y * y, axis=-1, keepdims=True) + EPS)
    o_ref[:, ch] = (y * nw_ref[:, ch]).astype(o_ref.dtype)


def _ssd_body(xbc_ref, z_ref, dt_ref, cw_ref, cb_ref, dtb_ref, alog_ref, dskip_ref, nw_ref, o_ref,
              state_ref, prev_ref, shift_ref):
    first = pl.program_id(1) == 0
    lc = CHUNK
    taps = SSD_CONV - 1

    @pl.when(first)
    def _():
        state_ref[...] = jnp.zeros_like(state_ref)
        prev_ref[...] = jnp.zeros_like(prev_ref)
        r = lax.broadcasted_iota(jnp.int32, (taps * lc, 2 * lc), 0)
        c = lax.broadcasted_iota(jnp.int32, (taps * lc, 2 * lc), 1)
        k = r // lc
        shift_ref[...] = jnp.where(c == lc + (r - k * lc) - (k + 1), 1.0, 0.0).astype(BF16)

    cur = xbc_ref[...]
    shifted = _dot(shift_ref[...], jnp.concatenate([prev_ref[...], cur], axis=0))
    prev_ref[...] = cur
    conv = cb_ref[...] + cw_ref[taps:taps + 1, :] * cur.astype(F32)
    for back in range(1, SSD_CONV):
        conv = conv + cw_ref[taps - back:taps - back + 1, :] * shifted[(back - 1) * lc:back * lc]
    conv = jax.nn.silu(conv)

    dt_all = jax.nn.softplus(dt_ref[...].astype(F32) + dtb_ref[...])
    a_all = dt_all * -jnp.exp(alog_ref[...])
    li = lax.broadcasted_iota(jnp.int32, (lc, lc), 0)
    si = lax.broadcasted_iota(jnp.int32, (lc, lc), 1)
    causal = li >= si
    a_cum_all = _dot_exact_lhs(jnp.where(causal, 1.0, 0.0).astype(BF16), a_all)
    eh = lax.broadcasted_iota(jnp.int32, (LANE, SSD_W), 0)
    ec = lax.broadcasted_iota(jnp.int32, (LANE, SSD_W), 1) // SSD_P
    spread = jnp.where(eh == ec, 1.0, 0.0).astype(BF16)
    for g in range(SSD_GROUPS):
        _ssd_group(g, conv, a_cum_all, dt_all, z_ref, dskip_ref, nw_ref, state_ref, o_ref, causal, spread)


def _ssd(proj, lw, layer, batch, seq):
    t = proj.shape[0]
    nc = seq // CHUNK
    glanes = SSD_GROUPS * LANE

    def rows(b, c):
        return pl.multiple_of((b * nc + c) * CHUNK, CHUNK)

    def window(width, offset):
        return pl.BlockSpec((pl.Element(CHUNK), pl.Element(width)), lambda b, c: (rows(b, c), offset))

    return pl.pallas_call(
        _ssd_body,
        grid=(batch, nc),
        in_specs=[
            window(SSD_CONV_DIM, OFF_XBC),
            window(SSD_INNER, OFF_Z),
            window(glanes, OFF_DT),
            pl.BlockSpec((None, SSD_CONV, SSD_CONV_DIM), lambda b, c: (layer, 0, 0)),
            pl.BlockSpec((None, 1, SSD_CONV_DIM), lambda b, c: (layer, 0, 0)),
            pl.BlockSpec((None, 1, glanes), lambda b, c: (layer, 0, 0)),
            pl.BlockSpec((None, 1, glanes), lambda b, c: (layer, 0, 0)),
            pl.BlockSpec((None, 1, SSD_INNER), lambda b, c: (layer, 0, 0)),
            pl.BlockSpec((None, 1, SSD_INNER), lambda b, c: (layer, 0, 0)),
        ],
        out_specs=pl.BlockSpec((CHUNK, SSD_INNER), lambda b, c: (b * nc + c, 0)),
        out_shape=jax.ShapeDtypeStruct((t, SSD_INNER), BF16),
        scratch_shapes=[
            pltpu.VMEM((SSD_N, SSD_INNER), F32),
            pltpu.VMEM((CHUNK, SSD_CONV_DIM), BF16),
            pltpu.VMEM(((SSD_CONV - 1) * CHUNK, 2 * CHUNK), BF16),
        ],
        compiler_params=_params(("parallel", "arbitrary")),
        name="ssd",
    )(proj, proj, proj, lw["conv_w"], lw["conv_b"], lw["dt_bias"], lw["a_log"], lw["d_skip"], lw["ssd_norm"])


def _ret_log_gamma(h):
    return float(np.log1p(-np.exp2(-5.0 - h)))


def _ret_body(q_ref, k_ref, v0_ref, v1_ref, g0_ref, g1_ref, nw_ref, o_ref, state_ref):
    lc = CHUNK
    half = RET_HEADS // 2
    v_refs = (v0_ref, v1_ref)
    g_refs = (g0_ref, g1_ref)

    @pl.when(pl.program_id(1) == 0)
    def _():
        state_ref[...] = jnp.zeros_like(state_ref)

    li = lax.broadcasted_iota(jnp.int32, (lc, lc), 0)
    si = lax.broadcasted_iota(jnp.int32, (lc, lc), 1)
    diff = (li - si).astype(F32)
    idx = lax.broadcasted_iota(jnp.int32, (lc, 1), 0).astype(F32)

    for h in range(RET_HEADS):
        log_g = _ret_log_gamma(h)
        dmat = jnp.where(diff >= 0, jnp.exp(diff * log_g), 0.0)
        k_dec = jnp.exp((lc - 1.0 - idx) * log_g)
        q_dec = jnp.exp((idx + 1.0) * log_g)
        q = q_ref[:, h * RET_DK:(h + 1) * RET_DK].astype(F32) * (RET_DK ** -0.5)
        k = k_ref[:, h * RET_DK:(h + 1) * RET_DK].astype(F32)
        vsl = slice((h % half) * RET_DV, (h % half + 1) * RET_DV)
        v = v_refs[h // half][:, vsl]
        scores = _dot_nt(q.astype(BF16), k.astype(BF16)) * dmat
        state = state_ref[h]
        y = _dot(scores.astype(BF16), v) + _dot((q * q_dec).astype(BF16), state.astype(BF16))
        chunk_kv = _dot((k * k_dec).T.astype(BF16), v)
        state_ref[h] = state * float(np.exp(lc * log_g)) + chunk_kv

        mu = jnp.mean(y, axis=-1, keepdims=True)
        yc = y - mu
        var = jnp.mean(yc * yc, axis=-1, keepdims=True)
        y = yc * lax.rsqrt(var + EPS) * nw_ref[h]
        gate = jax.nn.silu(g_refs[h // half][:, vsl].astype(F32))
        o_ref[:, h * RET_DV:(h + 1) * RET_DV] = (gate * y).astype(o_ref.dtype)


def _retention(proj, nw, layer, batch, seq):
    t = proj.shape[0]
    nc = seq // CHUNK
    hw = RET_V // 2
    q0 = OFF_RQK // RET_QK
    v0 = OFF_RV // hw
    g0 = OFF_RG // hw
    return pl.pallas_call(
        _ret_body,
        grid=(batch, nc),
        in_specs=[
            pl.BlockSpec((CHUNK, RET_QK), lambda b, c: (b * nc + c, q0)),
            pl.BlockSpec((CHUNK, RET_QK), lambda b, c: (b * nc + c, q0 + 1)),
            pl.BlockSpec((CHUNK, hw), lambda b, c: (b * nc + c, v0)),
            pl.BlockSpec((CHUNK, hw), lambda b, c: (b * nc + c, v0 + 1)),
            pl.BlockSpec((CHUNK, hw), lambda b, c: (b * nc + c, g0)),
            pl.BlockSpec((CHUNK, hw), lambda b, c: (b * nc + c, g0 + 1)),
            pl.BlockSpec((None, RET_HEADS, 1, RET_DV), lambda b, c: (layer, 0, 0, 0)),
        ],
        out_specs=pl.BlockSpec((CHUNK, RET_V), lambda b, c: (b * nc + c, 0)),
        out_shape=jax.ShapeDtypeStruct((t, RET_V), BF16),
        scratch_shapes=[pltpu.VMEM((RET_HEADS, RET_DK, RET_DV), F32)],
        compiler_params=_params(("parallel", "arbitrary")),
        name="retention",
    )(proj, proj, proj, proj, proj, proj, nw)


def _per_group_rows(seg, groups):
    depth, rows, d = seg.shape
    per = rows // groups
    seg = seg.reshape(depth, groups, per, d)
    seg = jnp.pad(seg, [(0, 0), (0, 0), (0, LANE - per), (0, 0)])
    return seg.reshape(depth, groups * LANE, d)


def _small_w_t(w_t):
    o_ng = IN_SPLITS[0] + IN_SPLITS[1]
    o_dt = o_ng + IN_SPLITS[2] + IN_SPLITS[3] + IN_SPLITS[4]
    ng = _per_group_rows(w_t[:, o_ng:o_ng + IN_SPLITS[2], :], NSA_GROUPS)
    dt = _per_group_rows(w_t[:, o_dt:o_dt + IN_SPLITS[5], :], SSD_GROUPS)
    pad = jnp.zeros((w_t.shape[0], N_SMALL - ng.shape[1] - dt.shape[1], w_t.shape[2]), w_t.dtype)
    return jnp.concatenate([ng, dt, pad], axis=1)


def _per_group_lanes(v, groups):
    depth, heads = v.shape
    per = heads // groups
    v = jnp.pad(v.reshape(depth, groups, per), [(0, 0), (0, 0), (0, LANE - per)])
    return v.reshape(depth, 1, groups * LANE)


def _layer(x, lw, layer, batch, seq):
    u = _rms(x, lw["norm_mix"], layer, BF16)
    proj = _inproj(u, lw["w_t"], lw["w_small_t"], layer)

    kv = proj[:, OFF_KV:OFF_KV + 2 * NSA_KV].reshape(batch, seq // CMP_STRIDE, CMP_STRIDE, 2, NSA_GROUPS, DH)
    xc = kv.transpose(0, 3, 4, 1, 2, 5).reshape(batch, 2, NSA_GROUPS, seq // CMP_STRIDE, CMP_STRIDE * DH)
    cmp = _compress(xc, lw["cmp_pe"][layer], lw["cmp_w1"][layer], lw["cmp_w2"][layer])

    y_nsa = _nsa(proj, cmp, batch, seq)
    y_ssd = _ssd(proj, lw, layer, batch, seq)
    y_ret = _retention(proj, lw["ret_norm"], layer, batch, seq)

    merged = _merge(y_nsa, y_ssd, y_ret, proj, lw["p_nsa"], lw["p_ssd"], lw["p_ret"], layer)
    x = _mm_res(merged, lw["w_out"], x, layer, 512)
    f = _rms(x, lw["norm_ffn"], layer, BF16)
    hidden = _ffn_up(f, lw["w_gate"], lw["w_up"], layer)
    return _mm_res(hidden, lw["w_down"], x, layer, 256)


def kernel(x, norm_mix, w_in, cmp_k_pe, cmp_k_w1, cmp_k_w2, cmp_v_pe, cmp_v_w1, cmp_v_w2, conv_w, conv_b, dt_bias, a_log, d_skip, ssd_norm, ret_norm, p_nsa, p_ssd, p_ret, w_out, norm_ffn, w_gate, w_up, w_down, norm_final):
    batch, seq, d = x.shape
    depth = w_in.shape[0]
    w_t = jnp.swapaxes(w_in, 1, 2)
    pe = jnp.stack([cmp_k_pe, cmp_v_pe], axis=1).reshape(depth, 2, 1, CMP_LEN * DH)
    lw = {
        "norm_mix": norm_mix[:, None, :],
        "w_t": w_t,
        "w_small_t": _small_w_t(w_t),
        "cmp_pe": jnp.pad(pe, [(0, 0), (0, 0), (0, 7), (0, 0)]).astype(BF16),
        "cmp_w1": jnp.stack([cmp_k_w1, cmp_v_w1], axis=1).astype(BF16),
        "cmp_w2": jnp.stack([cmp_k_w2, cmp_v_w2], axis=1).astype(BF16),
        "conv_w": conv_w,
        "conv_b": conv_b[:, None, :],
        "dt_bias": _per_group_lanes(dt_bias, SSD_GROUPS),
        "a_log": _per_group_lanes(a_log, SSD_GROUPS),
        "d_skip": jnp.repeat(d_skip, SSD_P, axis=-1)[:, None, :],
        "ssd_norm": ssd_norm[:, None, :],
        "ret_norm": ret_norm[:, :, None, :],
        "p_nsa": p_nsa, "p_ssd": p_ssd, "p_ret": p_ret, "w_out": w_out,
        "norm_ffn": norm_ffn[:, None, :],
        "w_gate": w_gate, "w_up": w_up, "w_down": w_down,
    }
    h = x.reshape(batch * seq, d)
    for layer in range(depth):
        h = _layer(h, lw, layer, batch, seq)
    return _rms(h, norm_final[None, None, :], 0, F32).reshape(batch, seq, d)
```

```python
import functools

import jax
import jax.numpy as jnp
import numpy as np
from jax import lax
from jax.experimental import pallas as pl
from jax.experimental.pallas import tpu as pltpu

F32 = jnp.float32
BF16 = jnp.bfloat16

D_MODEL = 2048
NSA_HEADS = 8
NSA_GROUPS = 2
NSA_J = NSA_HEADS // NSA_GROUPS
DH = 128
CMP_LEN = 32
CMP_STRIDE = 16
SLC_BLOCK = 64
SLC_TOPK = 8
WINDOW = 512
SSD_INNER = D_MODEL
SSD_P = 64
SSD_HEADS = SSD_INNER // SSD_P
SSD_GROUPS = 4
SSD_HG = SSD_HEADS // SSD_GROUPS
SSD_N = 128
SSD_CONV = 4
CHUNK = 128
RET_HEADS = 4
RET_DK = 128
RET_DV = 256
D_FF = -(-8 * D_MODEL // (3 * 256)) * 256
NSA_Q = NSA_HEADS * DH
NSA_KV = NSA_GROUPS * DH
SSD_BC = SSD_GROUPS * SSD_N
SSD_CONV_DIM = SSD_INNER + 2 * SSD_BC
RET_QK = RET_HEADS * RET_DK
RET_V = RET_HEADS * RET_DV
IN_SPLITS = (NSA_Q, 6 * NSA_KV, 3 * NSA_HEADS, SSD_INNER, SSD_CONV_DIM, SSD_HEADS,
             2 * RET_QK, RET_V, RET_V, 3 * D_MODEL)
D_IN = sum(IN_SPLITS)
NEG_INF = -1e30
FORCE_SCORE = 1e9
BELOW_ALL = -3e38
EPS = 1e-6

LANE = 128

OFF_Q = 0
OFF_KV = OFF_Q + NSA_Q
OFF_Z = OFF_KV + 6 * NSA_KV
OFF_XBC = OFF_Z + SSD_INNER
OFF_RQK = OFF_XBC + SSD_CONV_DIM
OFF_RV = OFF_RQK + 2 * RET_QK
OFF_RG = OFF_RV + RET_V
OFF_MG = OFF_RG + RET_V
OFF_NG = OFF_MG + 3 * D_MODEL
OFF_DT = OFF_NG + NSA_GROUPS * LANE
N_USED = OFF_DT + SSD_GROUPS * LANE
PROJ_TN = 512
N_PACK = -(-N_USED // PROJ_TN) * PROJ_TN
N_MAIN_BLOCKS = OFF_NG // PROJ_TN
N_SMALL = N_PACK - OFF_NG

SHIFT_B = IN_SPLITS[2]
SHIFT_C = IN_SPLITS[2] + IN_SPLITS[5]
BLK_B = OFF_Z // PROJ_TN
BLK_C = OFF_RQK // PROJ_TN

ROW_TILE = 2048
VMEM_LIMIT = 56 * 1024 * 1024


def _dot(a, b):
    return jnp.dot(a, b, preferred_element_type=F32)


def _dot_nt(a, b):
    return lax.dot_general(a, b, (((1,), (1,)), ((), ())), preferred_element_type=F32)


def _split3(x):
    hi = x.astype(BF16)
    r1 = x - hi.astype(F32)
    mid = r1.astype(BF16)
    lo = (r1 - mid.astype(F32)).astype(BF16)
    return hi, mid, lo


def _dot_exact_rhs(x, m_bf16):
    hi, mid, lo = _split3(x)
    return _dot(hi, m_bf16) + _dot(mid, m_bf16) + _dot(lo, m_bf16)


def _dot_exact_lhs(m_bf16, x):
    hi, mid, lo = _split3(x)
    return _dot(m_bf16, hi) + _dot(m_bf16, mid) + _dot(m_bf16, lo)


def _params(sem):
    return pltpu.CompilerParams(dimension_semantics=sem, vmem_limit_bytes=VMEM_LIMIT)


def _rms_body(x_ref, nw_ref, o_ref):
    slab = 128

    def body(r, carry):
        sl = pl.ds(pl.multiple_of(r * slab, slab), slab)
        x = x_ref[sl, :]
        ms = jnp.mean(x * x, axis=-1, keepdims=True)
        o_ref[sl, :] = (x * lax.rsqrt(ms + EPS) * nw_ref[...]).astype(o_ref.dtype)
        return carry

    lax.fori_loop(0, x_ref.shape[0] // slab, body, 0)


def _rms(x, nw, layer, out_dtype):
    t, d = x.shape
    tm = 1024
    return pl.pallas_call(
        _rms_body,
        grid=(t // tm,),
        in_specs=[pl.BlockSpec((tm, d), lambda i: (i, 0)),
                  pl.BlockSpec((None, 1, d), lambda i: (layer, 0, 0))],
        out_specs=pl.BlockSpec((tm, d), lambda i: (i, 0)),
        out_shape=jax.ShapeDtypeStruct((t, d), out_dtype),
        compiler_params=_params(("parallel",)),
        name="rmsnorm",
    )(x, nw)


CMP_BLK = OFF_KV // PROJ_TN
CMP_STREAMS = 2 * NSA_GROUPS
CMP_CHUNK_W = CMP_STRIDE * DH


def _inproj_body(u_ref, w_ref, ws_ref, o_ref, xc_ref, stage_ref):
    j = pl.program_id(1)

    @pl.when(j < N_MAIN_BLOCKS)
    def _():
        acc = _dot_nt(u_ref[...], w_ref[0].astype(BF16))
        o_ref[...] = acc.astype(o_ref.dtype)

        @pl.when(j == CMP_BLK)
        def _():
            nchunk = stage_ref.shape[1] // CMP_STRIDE
            for s in range(CMP_STREAMS):
                stage_ref[s] = acc[:, s * DH:(s + 1) * DH]
            for s in range(CMP_STREAMS):
                for l in range(CMP_STRIDE):
                    lo = s * CMP_CHUNK_W + l * DH
                    rows = stage_ref[s, pl.ds(l, nchunk, stride=CMP_STRIDE), :]
                    xc_ref[:, lo:lo + DH] = rows.astype(xc_ref.dtype)

    @pl.when(j >= N_MAIN_BLOCKS)
    def _():
        o_ref[...] = _dot_nt(u_ref[...], ws_ref[...].astype(BF16)).astype(o_ref.dtype)


def _inproj(u, w_t, w_small_t, layer):
    t, d = u.shape
    tm, tn = ROW_TILE, PROJ_TN
    assert OFF_KV == CMP_BLK * tn and CMP_STREAMS * DH == tn
    last_main = N_MAIN_BLOCKS - 1

    def w_index(i, j):
        jc = jnp.minimum(j, last_main)
        shift = jnp.where(jc >= BLK_C, SHIFT_C, jnp.where(jc >= BLK_B, SHIFT_B, 0))
        return (layer, pl.multiple_of(jc * tn + shift, 8), 0)

    return pl.pallas_call(
        _inproj_body,
        grid=(t // tm, N_PACK // tn),
        in_specs=[
            pl.BlockSpec((tm, d), lambda i, j: (i, 0)),
            pl.BlockSpec((pl.Element(1), pl.Element(tn), pl.Element(d)), w_index),
            pl.BlockSpec((None, tn, d), lambda i, j: (layer, jnp.maximum(j - N_MAIN_BLOCKS, 0), 0)),
        ],
        out_specs=[
            pl.BlockSpec((tm, tn), lambda i, j: (i, j)),
            pl.BlockSpec((tm // CMP_STRIDE, CMP_STREAMS * CMP_CHUNK_W), lambda i, j: (i, 0)),
        ],
        out_shape=[
            jax.ShapeDtypeStruct((t, N_PACK), BF16),
            jax.ShapeDtypeStruct((t // CMP_STRIDE, CMP_STREAMS * CMP_CHUNK_W), BF16),
        ],
        scratch_shapes=[pltpu.VMEM((CMP_STREAMS, tm, DH), F32)],
        compiler_params=_params(("parallel", "arbitrary")),
        name="inproj",
    )(u, w_t, w_small_t)


def _merge_body(yn_ref, ys_ref, yr_ref, gn_ref, gs_ref, gr_ref, pn_ref, ps_ref, pr_ref, o_ref):
    def gate(g_ref):
        return jax.nn.sigmoid(g_ref[...].astype(F32))

    acc = gate(gn_ref) * _dot(yn_ref[...], pn_ref[...].astype(BF16))
    acc = acc + gate(gs_ref) * _dot(ys_ref[...], ps_ref[...].astype(BF16))
    acc = acc + gate(gr_ref) * _dot(yr_ref[...], pr_ref[...].astype(BF16))
    o_ref[...] = acc.astype(o_ref.dtype)


def _merge(y_nsa, y_ssd, y_ret, proj, p_nsa, p_ssd, p_ret, layer):
    t = y_nsa.shape[0]
    tm, tn = 1024, 512
    g0 = OFF_MG // tn
    gstep = D_MODEL // tn
    return pl.pallas_call(
        _merge_body,
        grid=(t // tm, D_MODEL // tn),
        in_specs=[
            pl.BlockSpec((tm, NSA_Q), lambda i, j: (i, 0)),
            pl.BlockSpec((tm, SSD_INNER), lambda i, j: (i, 0)),
            pl.BlockSpec((tm, RET_V), lambda i, j: (i, 0)),
            pl.BlockSpec((tm, tn), lambda i, j: (i, g0 + j)),
            pl.BlockSpec((tm, tn), lambda i, j: (i, g0 + gstep + j)),
            pl.BlockSpec((tm, tn), lambda i, j: (i, g0 + 2 * gstep + j)),
            pl.BlockSpec((None, NSA_Q, tn), lambda i, j: (layer, 0, j)),
            pl.BlockSpec((None, SSD_INNER, tn), lambda i, j: (layer, 0, j)),
            pl.BlockSpec((None, RET_V, tn), lambda i, j: (layer, 0, j)),
        ],
        out_specs=pl.BlockSpec((tm, tn), lambda i, j: (i, j)),
        out_shape=jax.ShapeDtypeStruct((t, D_MODEL), BF16),
        compiler_params=_params(("parallel", "arbitrary")),
        name="merge",
    )(y_nsa, y_ssd, y_ret, proj, proj, proj, p_nsa, p_ssd, p_ret)


def _mm_res_body(a_ref, w_ref, r_ref, o_ref):
    o_ref[...] = r_ref[...] + _dot(a_ref[...], w_ref[...].astype(BF16))


def _mm_res(a, w, res, layer, tn):
    t, k = a.shape
    n = w.shape[2]
    tm = ROW_TILE if k <= D_MODEL else ROW_TILE // 2
    return pl.pallas_call(
        _mm_res_body,
        grid=(t // tm, n // tn),
        in_specs=[
            pl.BlockSpec((tm, k), lambda i, j: (i, 0)),
            pl.BlockSpec((None, k, tn), lambda i, j: (layer, 0, j)),
            pl.BlockSpec((tm, tn), lambda i, j: (i, j)),
        ],
        out_specs=pl.BlockSpec((tm, tn), lambda i, j: (i, j)),
        out_shape=jax.ShapeDtypeStruct((t, n), F32),
        compiler_params=_params(("parallel", "arbitrary")),
        name="mm_res",
    )(a, w, res)


def _ffn_up_body(f_ref, wg_ref, wu_ref, o_ref):
    f = f_ref[...]
    gate = _dot(f, wg_ref[...].astype(BF16))
    up = _dot(f, wu_ref[...].astype(BF16))
    o_ref[...] = (jax.nn.silu(gate) * up).astype(o_ref.dtype)


def _ffn_up(f, wg, wu, layer):
    t, d = f.shape
    n = wg.shape[2]
    tm, tn = ROW_TILE, 512
    return pl.pallas_call(
        _ffn_up_body,
        grid=(t // tm, n // tn),
        in_specs=[
            pl.BlockSpec((tm, d), lambda i, j: (i, 0)),
            pl.BlockSpec((None, d, tn), lambda i, j: (layer, 0, j)),
            pl.BlockSpec((None, d, tn), lambda i, j: (layer, 0, j)),
        ],
        out_specs=pl.BlockSpec((tm, tn), lambda i, j: (i, j)),
        out_shape=jax.ShapeDtypeStruct((t, n), BF16),
        compiler_params=_params(("parallel", "arbitrary")),
        name="ffn_up",
    )(f, wg, wu)


def _compress_body(x_ref, pe_ref, w1_ref, w2_ref, o_ref):
    half = (CMP_LEN // 2) * DH
    x = x_ref[...]
    a = _dot(x, w1_ref[:half, :])
    b = _dot(x, w1_ref[half:, :])
    bias = _dot(pe_ref[...], w1_ref[...])[0:1, :]
    nblk = x.shape[0]
    h = a + pltpu.roll(b, nblk - 1, 0) + bias
    h = jax.nn.gelu(h)
    o_ref[...] = _dot(h.astype(BF16), w2_ref[...]).astype(o_ref.dtype)


def _compress(xc, pe, w1, w2, batch):
    b, g = batch, NSA_GROUPS
    nchunk = xc.shape[0] // batch
    width = CMP_CHUNK_W
    return pl.pallas_call(
        _compress_body,
        grid=(b, 2, g),
        in_specs=[
            pl.BlockSpec((nchunk, width), lambda bi, m, gi: (bi, m * NSA_GROUPS + gi)),
            pl.BlockSpec((None, 8, CMP_LEN * DH), lambda bi, m, gi: (m, 0, 0)),
            pl.BlockSpec((None, CMP_LEN * DH, DH), lambda bi, m, gi: (m, 0, 0)),
            pl.BlockSpec((None, DH, DH), lambda bi, m, gi: (m, 0, 0)),
        ],
        out_specs=pl.BlockSpec((None, None, None, nchunk, DH), lambda bi, m, gi: (bi, m, gi, 0, 0)),
        out_shape=jax.ShapeDtypeStruct((b, 2, g, nchunk, DH), BF16),
        compiler_params=_params(("parallel", "arbitrary", "arbitrary")),
        name="nsa_compress",
    )(xc, pe, w1, w2)


NSA_TQ = 256
KEY_BLK = 256
VT_ROWS = DH + 16
NSA_COLS = NSA_J * NSA_TQ
SLC_KCHUNK = 256
WIN_SPAN = WINDOW + NSA_TQ
WIN_PAD_BLOCKS = WINDOW // KEY_BLK
AUG = 2 * DH
COL_BLK = SLC_BLOCK // 2
COL_OFF = COL_BLK + 1


def _nsa_body(q_ref, gate_ref, kc_ref, vc_ref, ks_ref, vs_ref, kw_ref, vw_ref, o_ref,
              ksa_ref, kwa_ref, vst_ref, vwt_ref, vct_ref, bw_ref, qsa_ref, qwa_ref, *, seq):
    g = pl.program_id(1)
    i = pl.program_id(2)
    tq = NSA_TQ
    cols = NSA_COLS
    n_slc = seq // SLC_BLOCK
    kb_rows = KEY_BLK
    nkb = seq // kb_rows

    cc = lax.broadcasted_iota(jnp.int32, (1, cols), 1)
    head = cc // tq
    tl = cc - head * tq
    slope_g = jnp.where(g == 0, 0.5, 0.5 ** (NSA_J + 1)).astype(F32)
    slope = slope_g * jnp.where(head == 0, 1.0, jnp.where(head == 1, 0.5, jnp.where(head == 2, 0.25, 0.125)))
    tl_f = tl.astype(F32)

    @pl.when(i == 0)
    def _init():
        def blk_body(kb, carry):
            r0 = pl.multiple_of(kb * kb_rows, kb_rows)
            rows = pl.ds(r0, kb_rows)
            key = r0 + lax.broadcasted_iota(jnp.int32, (kb_rows, DH), 0)
            col = lax.broadcasted_iota(jnp.int32, (kb_rows, DH), 1)
            blk = key // SLC_BLOCK
            off = key - blk * SLC_BLOCK
            extra = jnp.where(col == blk, 1.0, 0.0)
            extra = jnp.where(col == COL_BLK, blk.astype(F32), extra)
            extra = jnp.where(col == COL_OFF, off.astype(F32), extra)
            ksa_ref[rows, 0:DH] = ks_ref[rows, :]
            ksa_ref[rows, DH:AUG] = extra.astype(BF16)
            prow = pl.ds(pl.multiple_of(r0 + WINDOW, kb_rows), kb_rows)
            kwa_ref[prow, 0:DH] = kw_ref[rows, :]
            kwa_ref[prow, DH:AUG] = jnp.zeros((kb_rows, DH), BF16)
            orow = lax.broadcasted_iota(jnp.int32, (VT_ROWS - DH, kb_rows), 0)
            ones_row = jnp.where(orow == 0, 1.0, 0.0).astype(BF16)
            vst_ref[kb, 0:DH, :] = vs_ref[rows, :].astype(F32).T.astype(BF16)
            vst_ref[kb, DH:VT_ROWS, :] = ones_row
            vwt_ref[kb + WIN_PAD_BLOCKS, 0:DH, :] = vw_ref[rows, :].astype(F32).T.astype(BF16)
            vwt_ref[kb + WIN_PAD_BLOCKS, DH:VT_ROWS, :] = ones_row
            return carry

        lax.fori_loop(0, nkb, blk_body, 0)
        pcol = lax.broadcasted_iota(jnp.int32, (WINDOW, DH), 1)
        kwa_ref[0:WINDOW, 0:DH] = jnp.zeros((WINDOW, DH), BF16)
        kwa_ref[0:WINDOW, DH:AUG] = jnp.where(pcol == 0, 1.0, 0.0).astype(BF16)
        vwt_ref[0:WIN_PAD_BLOCKS] = jnp.zeros((WIN_PAD_BLOCKS, VT_ROWS, kb_rows), BF16)
        vct_ref[...] = vc_ref[...].astype(F32).T.astype(BF16)
        wk = lax.broadcasted_iota(jnp.int32, (WIN_SPAN, cols), 0)
        d = (WINDOW + tl - wk).astype(F32)
        bw_ref[...] = jnp.where((d >= 0) & (d < WINDOW), -slope * d, NEG_INF)

    q = q_ref[...].astype(F32) * (DH ** -0.5)
    qt = jnp.concatenate([q[:, j * DH:(j + 1) * DH].T for j in range(NSA_J)], axis=1).astype(BF16)
    qsa_ref[0:DH, :] = qt
    qwa_ref[0:DH, :] = qt
    er = lax.broadcasted_iota(jnp.int32, (DH, cols), 0)
    qwa_ref[DH:AUG, :] = jnp.where(er == 0, NEG_INF, 0.0).astype(BF16)

    ncmp = kc_ref.shape[0]
    n_idx = lax.broadcasted_iota(jnp.int32, (ncmp, 1), 0)
    rel_end = (n_idx * CMP_STRIDE + (CMP_LEN - 1)).astype(F32) - tl_f
    base = (i * tq).astype(F32)
    s = _dot(kc_ref[...], qt) + slope * rel_end
    s = jnp.where(rel_end <= base, s, NEG_INF)
    p = jnp.exp(s - jnp.max(s, axis=0, keepdims=True))
    p = p / jnp.sum(p, axis=0, keepdims=True)
    p = p * ((base + tl_f) >= (CMP_LEN - 1)).astype(F32)
    o_cmp = _dot(vct_ref[...], p.astype(BF16))

    psum = p[:, 0:tq]
    for j in range(1, NSA_J):
        psum = psum + p[:, j * tq:(j + 1) * tq]
    kk = lax.broadcasted_iota(jnp.int32, (LANE, ncmp), 0)
    nn = lax.broadcasted_iota(jnp.int32, (LANE, ncmp), 1)
    overlap = ((nn * CMP_STRIDE < (kk + 1) * SLC_BLOCK)
               & (nn * CMP_STRIDE + (CMP_LEN - 1) >= kk * SLC_BLOCK)
               & (kk < n_slc))
    imp = _dot_exact_lhs(jnp.where(overlap, 1.0, 0.0).astype(BF16), psum)[0:n_slc, :]
    blk = lax.broadcasted_iota(jnp.int32, (n_slc, tq), 0)
    blk_f = blk.astype(F32)
    cur = (i * tq + lax.broadcasted_iota(jnp.int32, (1, tq), 1)) // SLC_BLOCK
    val = jnp.where(blk > cur, NEG_INF, imp)
    val = jnp.where((blk == cur) | (blk == 0), FORCE_SCORE, val)
    sel = jnp.zeros((n_slc, tq), F32)
    for _ in range(min(SLC_TOPK, n_slc)):
        mx = jnp.max(val, axis=0, keepdims=True)
        idx = jnp.min(jnp.where(val == mx, blk_f, float(n_slc)), axis=0, keepdims=True)
        hit = blk_f == idx
        sel = jnp.where(hit, 1.0, sel)
        val = jnp.where(hit, BELOW_ALL, val)
    pen = (sel - 1.0) * (-NEG_INF)
    pen = jnp.concatenate([pen] * NSA_J, axis=1)
    er2 = lax.broadcasted_iota(jnp.int32, (DH - n_slc, cols), 0) + n_slc
    tail = jnp.where(er2 == COL_BLK, slope * SLC_BLOCK, jnp.where(er2 == COL_OFF, slope, 0.0))
    qsa_ref[DH:AUG, :] = jnp.concatenate([pen, tail], axis=0).astype(BF16)

    hw = 2 * LANE
    nhalf = cols // hw

    def slc_update(sc, c, state):
        m_run, acc = state
        m_new = jnp.maximum(m_run, jnp.max(sc, axis=0, keepdims=True))
        alpha = jnp.exp(m_run - m_new)
        prb = jnp.exp(sc - m_new).astype(BF16)
        return m_new, alpha * acc + _dot(vst_ref[c], prb)

    def slc_scores(c, h):
        start = pl.multiple_of(c * SLC_KCHUNK, SLC_KCHUNK)
        return _dot(ksa_ref[pl.ds(start, SLC_KCHUNK), :], qsa_ref[:, h * hw:(h + 1) * hw])

    def slc_step(c, carry):
        scs, states = carry
        nxt = tuple(slc_scores(c + 1, h) for h in range(nhalf))
        return nxt, tuple(slc_update(scs[h], c, states[h]) for h in range(nhalf))

    diag = (i * tq) // SLC_KCHUNK
    state0 = (jnp.full((1, hw), NEG_INF, F32), jnp.zeros((VT_ROWS, hw), F32))
    init = (tuple(slc_scores(0, h) for h in range(nhalf)), (state0,) * nhalf)
    scs, states = lax.fori_loop(0, diag, slc_step, init)
    krow = diag * SLC_KCHUNK + lax.broadcasted_iota(jnp.int32, (SLC_KCHUNK, 1), 0)
    o_halves = []
    for h in range(nhalf):
        causal = krow <= i * tq + tl[:, h * hw:(h + 1) * hw]
        _, acc_fin = slc_update(jnp.where(causal, scs[h], NEG_INF), diag, states[h])
        o_halves.append(acc_fin[0:DH] / acc_fin[DH:DH + 1])
    o_slc = jnp.concatenate(o_halves, axis=1)

    wstart = pl.multiple_of(i * tq, tq)
    sw = _dot(kwa_ref[pl.ds(wstart, WIN_SPAN), :], qwa_ref[...]) + bw_ref[...]
    pwb = jnp.exp(sw - jnp.max(sw, axis=0, keepdims=True)).astype(BF16)
    wblk = i * (tq // kb_rows)
    vwin = jnp.concatenate([vwt_ref[wblk + kb] for kb in range(WIN_SPAN // kb_rows)], axis=1)
    o_aug = _dot(vwin, pwb)
    o_win = o_aug[0:DH] / o_aug[DH:DH + 1]

    gt = jax.nn.sigmoid(gate_ref[...].astype(F32)).T

    def gate_row(branch):
        return jnp.concatenate([gt[3 * j + branch:3 * j + branch + 1, :] for j in range(NSA_J)], axis=1)

    out_t = gate_row(0) * o_cmp + gate_row(1) * o_slc + gate_row(2) * o_win
    o_ref[...] = jnp.concatenate([out_t[:, j * tq:(j + 1) * tq].T for j in range(NSA_J)], axis=1).astype(o_ref.dtype)


def _nsa(proj, cmp, batch, seq):
    t = proj.shape[0]
    nq = seq // NSA_TQ
    qw = NSA_J * DH
    kv0 = OFF_KV // DH

    def kv_spec(which):
        return pl.BlockSpec((seq, DH), lambda b, g, i: (b, kv0 + 2 * which + g))

    return pl.pallas_call(
        functools.partial(_nsa_body, seq=seq),
        grid=(batch, NSA_GROUPS, nq),
        in_specs=[
            pl.BlockSpec((NSA_TQ, qw), lambda b, g, i: (b * nq + i, g)),
            pl.BlockSpec((NSA_TQ, LANE), lambda b, g, i: (b * nq + i, OFF_NG // LANE + g)),
            pl.BlockSpec((None, None, None, seq // CMP_STRIDE, DH), lambda b, g, i: (b, 0, g, 0, 0)),
            pl.BlockSpec((None, None, None, seq // CMP_STRIDE, DH), lambda b, g, i: (b, 1, g, 0, 0)),
            kv_spec(2), kv_spec(3), kv_spec(4), kv_spec(5),
        ],
        out_specs=pl.BlockSpec((NSA_TQ, qw), lambda b, g, i: (b * nq + i, g)),
        out_shape=jax.ShapeDtypeStruct((t, NSA_Q), BF16),
        scratch_shapes=[
            pltpu.VMEM((seq, AUG), BF16),
            pltpu.VMEM((seq + WINDOW, AUG), BF16),
            pltpu.VMEM((seq // KEY_BLK, VT_ROWS, KEY_BLK), BF16),
            pltpu.VMEM((seq // KEY_BLK + WIN_PAD_BLOCKS, VT_ROWS, KEY_BLK), BF16),
            pltpu.VMEM((DH, seq // CMP_STRIDE), BF16),
            pltpu.VMEM((WIN_SPAN, NSA_COLS), F32),
            pltpu.VMEM((AUG, NSA_COLS), BF16),
            pltpu.VMEM((AUG, NSA_COLS), BF16),
        ],
        compiler_params=_params(("parallel", "arbitrary", "arbitrary")),
        name="nsa_attn",
    )(proj, proj, cmp, cmp, proj, proj, proj, proj)


SSD_W = SSD_HG * SSD_P


def _ssd_group(g, conv, a_cum_all, dt_all, z_ref, dskip_ref, nw_ref, state_ref, o_ref, causal, spread):
    lc = CHUNK
    ch = slice(g * SSD_W, (g + 1) * SSD_W)
    xs = conv[:, ch]
    bm = conv[:, SSD_INNER + g * SSD_N:SSD_INNER + (g + 1) * SSD_N]
    cm = conv[:, SSD_INNER + SSD_BC + g * SSD_N:SSD_INNER + SSD_BC + (g + 1) * SSD_N]
    dt = dt_all[:, g * LANE:(g + 1) * LANE]
    a_cum = a_cum_all[:, g * LANE:(g + 1) * LANE]
    a_cum_t = a_cum.T
    dt_x = _dot_exact_rhs(dt, spread)
    acum_x = _dot_exact_rhs(a_cum, spread)
    last_x = acum_x[lc - 1:lc, :]

    xdt = xs * dt_x
    xdt_bf = xdt.astype(BF16)
    cm_bf = cm.astype(BF16)
    cb = _dot_nt(cm_bf, bm.astype(BF16))

    y_heads = []
    for h in range(SSD_HG):
        col = acum_x[:, h * SSD_P:h * SSD_P + 1]
        rowv = a_cum_t[h:h + 1, :]
        decay = jnp.exp(jnp.where(causal, col - rowv, -jnp.inf))
        y_heads.append(_dot((cb * decay).astype(BF16), xdt_bf[:, h * SSD_P:(h + 1) * SSD_P]))
    y_diag = jnp.concatenate(y_heads, axis=-1)

    state = state_ref[:, ch]
    y_off = _dot(cm_bf, state.astype(BF16)) * jnp.exp(acum_x)
    to_end = jnp.exp(last_x - acum_x)
    chunk_state = _dot(bm.T.astype(BF16), (xdt * to_end).astype(BF16))
    state_ref[:, ch] = state * jnp.exp(last_x) + chunk_state

    y = y_diag + y_off + xs * dskip_ref[:, ch]
    y = y * jax.nn.silu(z_ref[:, ch].astype(F32))
    y = y * lax.rsqrt(jnp.mean(y * y, axis=-1, keepdims=True) + EPS)
    o_ref[:, ch] = (y * nw_ref[:, ch]).astype(o_ref.dtype)


def _ssd_body(xbc_ref, z_ref, dt_ref, cw_ref, cb_ref, dtb_ref, alog_ref, dskip_ref, nw_ref, o_ref,
              state_ref, prev_ref, shift_ref):
    first = pl.program_id(1) == 0
    lc = CHUNK
    taps = SSD_CONV - 1

    @pl.when(first)
    def _():
        state_ref[...] = jnp.zeros_like(state_ref)
        prev_ref[...] = jnp.zeros_like(prev_ref)
        r = lax.broadcasted_iota(jnp.int32, (taps * lc, 2 * lc), 0)
        c = lax.broadcasted_iota(jnp.int32, (taps * lc, 2 * lc), 1)
        k = r // lc
        shift_ref[...] = jnp.where(c == lc + (r - k * lc) - (k + 1), 1.0, 0.0).astype(BF16)

    cur = xbc_ref[...]
    shifted = _dot(shift_ref[...], jnp.concatenate([prev_ref[...], cur], axis=0))
    prev_ref[...] = cur
    conv = cb_ref[...] + cw_ref[taps:taps + 1, :] * cur.astype(F32)
    for back in range(1, SSD_CONV):
        conv = conv + cw_ref[taps - back:taps - back + 1, :] * shifted[(back - 1) * lc:back * lc]
    conv = jax.nn.silu(conv)

    dt_all = jax.nn.softplus(dt_ref[...].astype(F32) + dtb_ref[...])
    a_all = dt_all * -jnp.exp(alog_ref[...])
    li = lax.broadcasted_iota(jnp.int32, (lc, lc), 0)
    si = lax.broadcasted_iota(jnp.int32, (lc, lc), 1)
    causal = li >= si
    a_cum_all = _dot_exact_lhs(jnp.where(causal, 1.0, 0.0).astype(BF16), a_all)
    eh = lax.broadcasted_iota(jnp.int32, (LANE, SSD_W), 0)
    ec = lax.broadcasted_iota(jnp.int32, (LANE, SSD_W), 1) // SSD_P
    spread = jnp.where(eh == ec, 1.0, 0.0).astype(BF16)
    for g in range(SSD_GROUPS):
        _ssd_group(g, conv, a_cum_all, dt_all, z_ref, dskip_ref, nw_ref, state_ref, o_ref, causal, spread)


def _ssd(proj, lw, layer, batch, seq):
    t = proj.shape[0]
    nc = seq // CHUNK
    glanes = SSD_GROUPS * LANE

    def rows(b, c):
        return pl.multiple_of((b * nc + c) * CHUNK, CHUNK)

    def window(width, offset):
        return pl.BlockSpec((pl.Element(CHUNK), pl.Element(width)), lambda b, c: (rows(b, c), offset))

    return pl.pallas_call(
        _ssd_body,
        grid=(batch, nc),
        in_specs=[
            window(SSD_CONV_DIM, OFF_XBC),
            window(SSD_INNER, OFF_Z),
            window(glanes, OFF_DT),
            pl.BlockSpec((None, SSD_CONV, SSD_CONV_DIM), lambda b, c: (layer, 0, 0)),
            pl.BlockSpec((None, 1, SSD_CONV_DIM), lambda b, c: (layer, 0, 0)),
            pl.BlockSpec((None, 1, glanes), lambda b, c: (layer, 0, 0)),
            pl.BlockSpec((None, 1, glanes), lambda b, c: (layer, 0, 0)),
            pl.BlockSpec((None, 1, SSD_INNER), lambda b, c: (layer, 0, 0)),
            pl.BlockSpec((None, 1, SSD_INNER), lambda b, c: (layer, 0, 0)),
        ],
        out_specs=pl.BlockSpec((CHUNK, SSD_INNER), lambda b, c: (b * nc + c, 0)),
        out_shape=jax.ShapeDtypeStruct((t, SSD_INNER), BF16),
        scratch_shapes=[
            pltpu.VMEM((SSD_N, SSD_INNER), F32),
            pltpu.VMEM((CHUNK, SSD_CONV_DIM), BF16),
            pltpu.VMEM(((SSD_CONV - 1) * CHUNK, 2 * CHUNK), BF16),
        ],
        compiler_params=_params(("parallel", "arbitrary")),
        name="ssd",
    )(proj, proj, proj, lw["conv_w"], lw["conv_b"], lw["dt_bias"], lw["a_log"], lw["d_skip"], lw["ssd_norm"])


def _ret_log_gamma(h):
    return float(np.log1p(-np.exp2(-5.0 - h)))


def _ret_body(q_ref, k_ref, v0_ref, v1_ref, g0_ref, g1_ref, nw_ref, o_ref, state_ref):
    lc = CHUNK
    half = RET_HEADS // 2
    v_refs = (v0_ref, v1_ref)
    g_refs = (g0_ref, g1_ref)

    @pl.when(pl.program_id(1) == 0)
    def _():
        state_ref[...] = jnp.zeros_like(state_ref)

    li = lax.broadcasted_iota(jnp.int32, (lc, lc), 0)
    si = lax.broadcasted_iota(jnp.int32, (lc, lc), 1)
    diff = (li - si).astype(F32)
    idx = lax.broadcasted_iota(jnp.int32, (lc, 1), 0).astype(F32)

    for h in range(RET_HEADS):
        log_g = _ret_log_gamma(h)
        dmat = jnp.where(diff >= 0, jnp.exp(diff * log_g), 0.0)
        k_dec = jnp.exp((lc - 1.0 - idx) * log_g)
        q_dec = jnp.exp((idx + 1.0) * log_g)
        q = q_ref[:, h * RET_DK:(h + 1) * RET_DK].astype(F32) * (RET_DK ** -0.5)
        k = k_ref[:, h * RET_DK:(h + 1) * RET_DK].astype(F32)
        vsl = slice((h % half) * RET_DV, (h % half + 1) * RET_DV)
        v = v_refs[h // half][:, vsl]
        scores = _dot_nt(q.astype(BF16), k.astype(BF16)) * dmat
        state = state_ref[h]
        y = _dot(scores.astype(BF16), v) + _dot((q * q_dec).astype(BF16), state.astype(BF16))
        chunk_kv = _dot((k * k_dec).T.astype(BF16), v)
        state_ref[h] = state * float(np.exp(lc * log_g)) + chunk_kv

        mu = jnp.mean(y, axis=-1, keepdims=True)
        yc = y - mu
        var = jnp.mean(yc * yc, axis=-1, keepdims=True)
        y = yc * lax.rsqrt(var + EPS) * nw_ref[h]
        gate = jax.nn.silu(g_refs[h // half][:, vsl].astype(F32))
        o_ref[:, h * RET_DV:(h + 1) * RET_DV] = (gate * y).astype(o_ref.dtype)


def _retention(proj, nw, layer, batch, seq):
    t = proj.shape[0]
    nc = seq // CHUNK
    hw = RET_V // 2
    q0 = OFF_RQK // RET_QK
    v0 = OFF_RV // hw
    g0 = OFF_RG // hw
    return pl.pallas_call(
        _ret_body,
        grid=(batch, nc),
        in_specs=[
            pl.BlockSpec((CHUNK, RET_QK), lambda b, c: (b * nc + c, q0)),
            pl.BlockSpec((CHUNK, RET_QK), lambda b, c: (b * nc + c, q0 + 1)),
            pl.BlockSpec((CHUNK, hw), lambda b, c: (b * nc + c, v0)),
            pl.BlockSpec((CHUNK, hw), lambda b, c: (b * nc + c, v0 + 1)),
            pl.BlockSpec((CHUNK, hw), lambda b, c: (b * nc + c, g0)),
            pl.BlockSpec((CHUNK, hw), lambda b, c: (b * nc + c, g0 + 1)),
            pl.BlockSpec((None, RET_HEADS, 1, RET_DV), lambda b, c: (layer, 0, 0, 0)),
        ],
        out_specs=pl.BlockSpec((CHUNK, RET_V), lambda b, c: (b * nc + c, 0)),
        out_shape=jax.ShapeDtypeStruct((t, RET_V), BF16),
        scratch_shapes=[pltpu.VMEM((RET_HEADS, RET_DK, RET_DV), F32)],
        compiler_params=_params(("parallel", "arbitrary")),
        name="retention",
    )(proj, proj, proj, proj, proj, proj, nw)


def _per_group_rows(seg, groups):
    depth, rows, d = seg.shape
    per = rows // groups
    seg = seg.reshape(depth, groups, per, d)
    seg = jnp.pad(seg, [(0, 0), (0, 0), (0, LANE - per), (0, 0)])
    return seg.reshape(depth, groups * LANE, d)


def _small_w_t(w_t):
    o_ng = IN_SPLITS[0] + IN_SPLITS[1]
    o_dt = o_ng + IN_SPLITS[2] + IN_SPLITS[3] + IN_SPLITS[4]
    ng = _per_group_rows(w_t[:, o_ng:o_ng + IN_SPLITS[2], :], NSA_GROUPS)
    dt = _per_group_rows(w_t[:, o_dt:o_dt + IN_SPLITS[5], :], SSD_GROUPS)
    pad = jnp.zeros((w_t.shape[0], N_SMALL - ng.shape[1] - dt.shape[1], w_t.shape[2]), w_t.dtype)
    return jnp.concatenate([ng, dt, pad], axis=1)


def _per_group_lanes(v, groups):
    depth, heads = v.shape
    per = heads // groups
    v = jnp.pad(v.reshape(depth, groups, per), [(0, 0), (0, 0), (0, LANE - per)])
    return v.reshape(depth, 1, groups * LANE)


def _layer(x, lw, layer, batch, seq):
    u = _rms(x, lw["norm_mix"], layer, BF16)
    proj, xc = _inproj(u, lw["w_t"], lw["w_small_t"], layer)
    cmp = _compress(xc, lw["cmp_pe"][layer], lw["cmp_w1"][layer], lw["cmp_w2"][layer], batch)

    y_nsa = _nsa(proj, cmp, batch, seq)
    y_ssd = _ssd(proj, lw, layer, batch, seq)
    y_ret = _retention(proj, lw["ret_norm"], layer, batch, seq)

    merged = _merge(y_nsa, y_ssd, y_ret, proj, lw["p_nsa"], lw["p_ssd"], lw["p_ret"], layer)
    x = _mm_res(merged, lw["w_out"], x, layer, 512)
    f = _rms(x, lw["norm_ffn"], layer, BF16)
    hidden = _ffn_up(f, lw["w_gate"], lw["w_up"], layer)
    return _mm_res(hidden, lw["w_down"], x, layer, 256)


def kernel(x, norm_mix, w_in, cmp_k_pe, cmp_k_w1, cmp_k_w2, cmp_v_pe, cmp_v_w1, cmp_v_w2, conv_w, conv_b, dt_bias, a_log, d_skip, ssd_norm, ret_norm, p_nsa, p_ssd, p_ret, w_out, norm_ffn, w_gate, w_up, w_down, norm_final):
    batch, seq, d = x.shape
    depth = w_in.shape[0]
    w_t = jnp.swapaxes(w_in, 1, 2)
    pe = jnp.stack([cmp_k_pe, cmp_v_pe], axis=1).reshape(depth, 2, 1, CMP_LEN * DH)
    lw = {
        "norm_mix": norm_mix[:, None, :],
        "w_t": w_t,
        "w_small_t": _small_w_t(w_t),
        "cmp_pe": jnp.pad(pe, [(0, 0), (0, 0), (0, 7), (0, 0)]).astype(BF16),
        "cmp_w1": jnp.stack([cmp_k_w1, cmp_v_w1], axis=1).astype(BF16),
        "cmp_w2": jnp.stack([cmp_k_w2, cmp_v_w2], axis=1).astype(BF16),
        "conv_w": conv_w,
        "conv_b": conv_b[:, None, :],
        "dt_bias": _per_group_lanes(dt_bias, SSD_GROUPS),
        "a_log": _per_group_lanes(a_log, SSD_GROUPS),
        "d_skip": jnp.repeat(d_skip, SSD_P, axis=-1)[:, None, :],
        "ssd_norm": ssd_norm[:, None, :],
        "ret_norm": ret_norm[:, :, None, :],
        "p_nsa": p_nsa, "p_ssd": p_ssd, "p_ret": p_ret, "w_out": w_out,
        "norm_ffn": norm_ffn[:, None, :],
        "w_gate": w_gate, "w_up": w_up, "w_down": w_down,
    }
    h = x.reshape(batch * seq, d)
    for layer in range(depth):
        h = _layer(h, lw, layer, batch, seq)
    return _rms(h, norm_final[None, None, :], 0, F32).reshape(batch, seq, d)
```

```python
import functools

import jax
import jax.numpy as jnp
import numpy as np
from jax import lax
from jax.experimental import pallas as pl
from jax.experimental.pallas import tpu as pltpu

F32 = jnp.float32
BF16 = jnp.bfloat16

D_MODEL = 2048
NSA_HEADS = 8
NSA_GROUPS = 2
NSA_J = NSA_HEADS // NSA_GROUPS
DH = 128
CMP_LEN = 32
CMP_STRIDE = 16
SLC_BLOCK = 64
SLC_TOPK = 8
WINDOW = 512
SSD_INNER = D_MODEL
SSD_P = 64
SSD_HEADS = SSD_INNER // SSD_P
SSD_GROUPS = 4
SSD_HG = SSD_HEADS // SSD_GROUPS
SSD_N = 128
SSD_CONV = 4
CHUNK = 128
RET_HEADS = 4
RET_DK = 128
RET_DV = 256
D_FF = -(-8 * D_MODEL // (3 * 256)) * 256
NSA_Q = NSA_HEADS * DH
NSA_KV = NSA_GROUPS * DH
SSD_BC = SSD_GROUPS * SSD_N
SSD_CONV_DIM = SSD_INNER + 2 * SSD_BC
RET_QK = RET_HEADS * RET_DK
RET_V = RET_HEADS * RET_DV
IN_SPLITS = (NSA_Q, 6 * NSA_KV, 3 * NSA_HEADS, SSD_INNER, SSD_CONV_DIM, SSD_HEADS,
             2 * RET_QK, RET_V, RET_V, 3 * D_MODEL)
D_IN = sum(IN_SPLITS)
NEG_INF = -1e30
FORCE_SCORE = 1e9
BELOW_ALL = -3e38
EPS = 1e-6

LANE = 128

OFF_Q = 0
OFF_KV = OFF_Q + NSA_Q
OFF_Z = OFF_KV + 6 * NSA_KV
OFF_XBC = OFF_Z + SSD_INNER
OFF_RQK = OFF_XBC + SSD_CONV_DIM
OFF_RV = OFF_RQK + 2 * RET_QK
OFF_RG = OFF_RV + RET_V
OFF_MG = OFF_RG + RET_V
OFF_NG = OFF_MG + 3 * D_MODEL
OFF_DT = OFF_NG + NSA_GROUPS * LANE
N_USED = OFF_DT + LANE
PROJ_TN = 512
N_PACK = -(-N_USED // PROJ_TN) * PROJ_TN
N_MAIN_BLOCKS = OFF_NG // PROJ_TN
N_SMALL = N_PACK - OFF_NG

SHIFT_B = IN_SPLITS[2]
SHIFT_C = IN_SPLITS[2] + IN_SPLITS[5]
BLK_B = OFF_Z // PROJ_TN
BLK_C = OFF_RQK // PROJ_TN

ROW_TILE = 2048
VMEM_LIMIT = 56 * 1024 * 1024


def _dot(a, b):
    return jnp.dot(a, b, preferred_element_type=F32)


def _dot_nt(a, b):
    return lax.dot_general(a, b, (((1,), (1,)), ((), ())), preferred_element_type=F32)


def _split3(x):
    hi = x.astype(BF16)
    r1 = x - hi.astype(F32)
    mid = r1.astype(BF16)
    lo = (r1 - mid.astype(F32)).astype(BF16)
    return hi, mid, lo


def _dot_exact_rhs(x, m_bf16):
    hi, mid, lo = _split3(x)
    return _dot(hi, m_bf16) + _dot(mid, m_bf16) + _dot(lo, m_bf16)


def _dot_exact_lhs(m_bf16, x):
    hi, mid, lo = _split3(x)
    return _dot(m_bf16, hi) + _dot(m_bf16, mid) + _dot(m_bf16, lo)


def _params(sem):
    return pltpu.CompilerParams(dimension_semantics=sem, vmem_limit_bytes=VMEM_LIMIT)


def _rms_body(x_ref, nw_ref, o_ref):
    slab = 128

    def body(r, carry):
        sl = pl.ds(pl.multiple_of(r * slab, slab), slab)
        x = x_ref[sl, :]
        ms = jnp.mean(x * x, axis=-1, keepdims=True)
        o_ref[sl, :] = (x * lax.rsqrt(ms + EPS) * nw_ref[...]).astype(o_ref.dtype)
        return carry

    lax.fori_loop(0, x_ref.shape[0] // slab, body, 0)


def _rms(x, nw, layer, out_dtype):
    t, d = x.shape
    tm = 1024
    return pl.pallas_call(
        _rms_body,
        grid=(t // tm,),
        in_specs=[pl.BlockSpec((tm, d), lambda i: (i, 0)),
                  pl.BlockSpec((None, 1, d), lambda i: (layer, 0, 0))],
        out_specs=pl.BlockSpec((tm, d), lambda i: (i, 0)),
        out_shape=jax.ShapeDtypeStruct((t, d), out_dtype),
        compiler_params=_params(("parallel",)),
        name="rmsnorm",
    )(x, nw)


CMP_BLK = OFF_KV // PROJ_TN
CMP_STREAMS = 2 * NSA_GROUPS
CMP_CHUNK_W = CMP_STRIDE * DH


def _inproj_body(u_ref, w_ref, ws_ref, o_ref, xc_ref, stage_ref):
    j = pl.program_id(1)

    @pl.when(j < N_MAIN_BLOCKS)
    def _():
        acc = _dot_nt(u_ref[...], w_ref[0].astype(BF16))
        o_ref[...] = acc.astype(o_ref.dtype)

        @pl.when(j == CMP_BLK)
        def _():
            nchunk = stage_ref.shape[1] // CMP_STRIDE
            for s in range(CMP_STREAMS):
                stage_ref[s] = acc[:, s * DH:(s + 1) * DH]
            for s in range(CMP_STREAMS):
                for l in range(CMP_STRIDE):
                    lo = s * CMP_CHUNK_W + l * DH
                    rows = stage_ref[s, pl.ds(l, nchunk, stride=CMP_STRIDE), :]
                    xc_ref[:, lo:lo + DH] = rows.astype(xc_ref.dtype)

    @pl.when(j >= N_MAIN_BLOCKS)
    def _():
        o_ref[...] = _dot_nt(u_ref[...], ws_ref[...].astype(BF16)).astype(o_ref.dtype)


def _inproj(u, w_t, w_small_t, layer):
    t, d = u.shape
    tm, tn = ROW_TILE, PROJ_TN
    assert OFF_KV == CMP_BLK * tn and CMP_STREAMS * DH == tn
    last_main = N_MAIN_BLOCKS - 1

    def w_index(i, j):
        jc = jnp.minimum(j, last_main)
        shift = jnp.where(jc >= BLK_C, SHIFT_C, jnp.where(jc >= BLK_B, SHIFT_B, 0))
        return (layer, pl.multiple_of(jc * tn + shift, 8), 0)

    return pl.pallas_call(
        _inproj_body,
        grid=(t // tm, N_PACK // tn),
        in_specs=[
            pl.BlockSpec((tm, d), lambda i, j: (i, 0)),
            pl.BlockSpec((pl.Element(1), pl.Element(tn), pl.Element(d)), w_index),
            pl.BlockSpec((None, tn, d), lambda i, j: (layer, jnp.maximum(j - N_MAIN_BLOCKS, 0), 0)),
        ],
        out_specs=[
            pl.BlockSpec((tm, tn), lambda i, j: (i, j)),
            pl.BlockSpec((tm // CMP_STRIDE, CMP_STREAMS * CMP_CHUNK_W), lambda i, j: (i, 0)),
        ],
        out_shape=[
            jax.ShapeDtypeStruct((t, N_PACK), BF16),
            jax.ShapeDtypeStruct((t // CMP_STRIDE, CMP_STREAMS * CMP_CHUNK_W), BF16),
        ],
        scratch_shapes=[pltpu.VMEM((CMP_STREAMS, tm, DH), F32)],
        compiler_params=_params(("parallel", "arbitrary")),
        name="inproj",
    )(u, w_t, w_small_t)


def _merge_body(yn_ref, ys_ref, yr_ref, gn_ref, gs_ref, gr_ref, pn_ref, ps_ref, pr_ref, o_ref):
    def gate(g_ref):
        return jax.nn.sigmoid(g_ref[...].astype(F32))

    acc = gate(gn_ref) * _dot(yn_ref[...], pn_ref[...].astype(BF16))
    acc = acc + gate(gs_ref) * _dot(ys_ref[...], ps_ref[...].astype(BF16))
    acc = acc + gate(gr_ref) * _dot(yr_ref[...], pr_ref[...].astype(BF16))
    o_ref[...] = acc.astype(o_ref.dtype)


def _merge(y_nsa, y_ssd, y_ret, proj, p_nsa, p_ssd, p_ret, layer):
    t = y_nsa.shape[0]
    tm, tn = 1024, 512
    g0 = OFF_MG // tn
    gstep = D_MODEL // tn
    return pl.pallas_call(
        _merge_body,
        grid=(t // tm, D_MODEL // tn),
        in_specs=[
            pl.BlockSpec((tm, NSA_Q), lambda i, j: (i, 0)),
            pl.BlockSpec((tm, SSD_INNER), lambda i, j: (i, 0)),
            pl.BlockSpec((tm, RET_V), lambda i, j: (i, 0)),
            pl.BlockSpec((tm, tn), lambda i, j: (i, g0 + j)),
            pl.BlockSpec((tm, tn), lambda i, j: (i, g0 + gstep + j)),
            pl.BlockSpec((tm, tn), lambda i, j: (i, g0 + 2 * gstep + j)),
            pl.BlockSpec((None, NSA_Q, tn), lambda i, j: (layer, 0, j)),
            pl.BlockSpec((None, SSD_INNER, tn), lambda i, j: (layer, 0, j)),
            pl.BlockSpec((None, RET_V, tn), lambda i, j: (layer, 0, j)),
        ],
        out_specs=pl.BlockSpec((tm, tn), lambda i, j: (i, j)),
        out_shape=jax.ShapeDtypeStruct((t, D_MODEL), BF16),
        compiler_params=_params(("parallel", "arbitrary")),
        name="merge",
    )(y_nsa, y_ssd, y_ret, proj, proj, proj, p_nsa, p_ssd, p_ret)


def _mm_res_body(a_ref, w_ref, r_ref, o_ref):
    o_ref[...] = r_ref[...] + _dot(a_ref[...], w_ref[...].astype(BF16))


def _mm_res(a, w, res, layer, tn):
    t, k = a.shape
    n = w.shape[2]
    tm = ROW_TILE if k <= D_MODEL else ROW_TILE // 2
    return pl.pallas_call(
        _mm_res_body,
        grid=(t // tm, n // tn),
        in_specs=[
            pl.BlockSpec((tm, k), lambda i, j: (i, 0)),
            pl.BlockSpec((None, k, tn), lambda i, j: (layer, 0, j)),
            pl.BlockSpec((tm, tn), lambda i, j: (i, j)),
        ],
        out_specs=pl.BlockSpec((tm, tn), lambda i, j: (i, j)),
        out_shape=jax.ShapeDtypeStruct((t, n), F32),
        compiler_params=_params(("parallel", "arbitrary")),
        name="mm_res",
    )(a, w, res)


def _ffn_up_body(f_ref, wg_ref, wu_ref, o_ref):
    f = f_ref[...]
    gate = _dot(f, wg_ref[...].astype(BF16))
    up = _dot(f, wu_ref[...].astype(BF16))
    o_ref[...] = (jax.nn.silu(gate) * up).astype(o_ref.dtype)


def _ffn_up(f, wg, wu, layer):
    t, d = f.shape
    n = wg.shape[2]
    tm, tn = ROW_TILE, 512
    return pl.pallas_call(
        _ffn_up_body,
        grid=(t // tm, n // tn),
        in_specs=[
            pl.BlockSpec((tm, d), lambda i, j: (i, 0)),
            pl.BlockSpec((None, d, tn), lambda i, j: (layer, 0, j)),
            pl.BlockSpec((None, d, tn), lambda i, j: (layer, 0, j)),
        ],
        out_specs=pl.BlockSpec((tm, tn), lambda i, j: (i, j)),
        out_shape=jax.ShapeDtypeStruct((t, n), BF16),
        compiler_params=_params(("parallel", "arbitrary")),
        name="ffn_up",
    )(f, wg, wu)


def _compress_body(x_ref, pe_ref, w1_ref, w2_ref, o_ref):
    half = (CMP_LEN // 2) * DH
    x = x_ref[...]
    a = _dot(x, w1_ref[:half, :])
    b = _dot(x, w1_ref[half:, :])
    bias = _dot(pe_ref[...], w1_ref[...])[0:1, :]
    nblk = x.shape[0]
    h = a + pltpu.roll(b, nblk - 1, 0) + bias
    h = jax.nn.gelu(h)
    o_ref[...] = _dot(h.astype(BF16), w2_ref[...]).astype(o_ref.dtype)


def _compress(xc, pe, w1, w2, batch):
    b, g = batch, NSA_GROUPS
    nchunk = xc.shape[0] // batch
    width = CMP_CHUNK_W
    return pl.pallas_call(
        _compress_body,
        grid=(b, 2, g),
        in_specs=[
            pl.BlockSpec((nchunk, width), lambda bi, m, gi: (bi, m * NSA_GROUPS + gi)),
            pl.BlockSpec((None, 8, CMP_LEN * DH), lambda bi, m, gi: (m, 0, 0)),
            pl.BlockSpec((None, CMP_LEN * DH, DH), lambda bi, m, gi: (m, 0, 0)),
            pl.BlockSpec((None, DH, DH), lambda bi, m, gi: (m, 0, 0)),
        ],
        out_specs=pl.BlockSpec((None, None, None, nchunk, DH), lambda bi, m, gi: (bi, m, gi, 0, 0)),
        out_shape=jax.ShapeDtypeStruct((b, 2, g, nchunk, DH), BF16),
        compiler_params=_params(("parallel", "arbitrary", "arbitrary")),
        name="nsa_compress",
    )(xc, pe, w1, w2)


NSA_TQ = 256
KEY_BLK = 256
VT_ROWS = DH + 16
NSA_COLS = NSA_J * NSA_TQ
SLC_KCHUNK = 256
WIN_SPAN = WINDOW + NSA_TQ
WIN_PAD_BLOCKS = WINDOW // KEY_BLK
AUG = 2 * DH
COL_BLK = SLC_BLOCK // 2
COL_OFF = COL_BLK + 1


def _nsa_body(q_ref, gate_ref, kc_ref, vc_ref, ks_ref, vs_ref, kw_ref, vw_ref, o_ref,
              ksa_ref, kwa_ref, vst_ref, vwt_ref, vct_ref, bw_ref, qsa_ref, qwa_ref, *, seq):
    g = pl.program_id(1)
    i = pl.program_id(2)
    tq = NSA_TQ
    cols = NSA_COLS
    n_slc = seq // SLC_BLOCK
    kb_rows = KEY_BLK
    nkb = seq // kb_rows

    cc = lax.broadcasted_iota(jnp.int32, (1, cols), 1)
    head = cc // tq
    tl = cc - head * tq
    slope_g = jnp.where(g == 0, 0.5, 0.5 ** (NSA_J + 1)).astype(F32)
    slope = slope_g * jnp.where(head == 0, 1.0, jnp.where(head == 1, 0.5, jnp.where(head == 2, 0.25, 0.125)))
    tl_f = tl.astype(F32)

    @pl.when(i == 0)
    def _init():
        def blk_body(kb, carry):
            r0 = pl.multiple_of(kb * kb_rows, kb_rows)
            rows = pl.ds(r0, kb_rows)
            key = r0 + lax.broadcasted_iota(jnp.int32, (kb_rows, DH), 0)
            col = lax.broadcasted_iota(jnp.int32, (kb_rows, DH), 1)
            blk = key // SLC_BLOCK
            off = key - blk * SLC_BLOCK
            extra = jnp.where(col == blk, 1.0, 0.0)
            extra = jnp.where(col == COL_BLK, blk.astype(F32), extra)
            extra = jnp.where(col == COL_OFF, off.astype(F32), extra)
            ksa_ref[rows, 0:DH] = ks_ref[rows, :]
            ksa_ref[rows, DH:AUG] = extra.astype(BF16)
            prow = pl.ds(pl.multiple_of(r0 + WINDOW, kb_rows), kb_rows)
            kwa_ref[prow, 0:DH] = kw_ref[rows, :]
            kwa_ref[prow, DH:AUG] = jnp.zeros((kb_rows, DH), BF16)
            orow = lax.broadcasted_iota(jnp.int32, (VT_ROWS - DH, kb_rows), 0)
            ones_row = jnp.where(orow == 0, 1.0, 0.0).astype(BF16)
            vst_ref[kb, 0:DH, :] = vs_ref[rows, :].astype(F32).T.astype(BF16)
            vst_ref[kb, DH:VT_ROWS, :] = ones_row
            vwt_ref[kb + WIN_PAD_BLOCKS, 0:DH, :] = vw_ref[rows, :].astype(F32).T.astype(BF16)
            vwt_ref[kb + WIN_PAD_BLOCKS, DH:VT_ROWS, :] = ones_row
            return carry

        lax.fori_loop(0, nkb, blk_body, 0)
        pcol = lax.broadcasted_iota(jnp.int32, (WINDOW, DH), 1)
        kwa_ref[0:WINDOW, 0:DH] = jnp.zeros((WINDOW, DH), BF16)
        kwa_ref[0:WINDOW, DH:AUG] = jnp.where(pcol == 0, 1.0, 0.0).astype(BF16)
        vwt_ref[0:WIN_PAD_BLOCKS] = jnp.zeros((WIN_PAD_BLOCKS, VT_ROWS, kb_rows), BF16)
        vct_ref[...] = vc_ref[...].astype(F32).T.astype(BF16)
        wk = lax.broadcasted_iota(jnp.int32, (WIN_SPAN, cols), 0)
        d = (WINDOW + tl - wk).astype(F32)
        bw_ref[...] = jnp.where((d >= 0) & (d < WINDOW), -slope * d, NEG_INF)

    q = q_ref[...].astype(F32) * (DH ** -0.5)
    qt = jnp.concatenate([q[:, j * DH:(j + 1) * DH].T for j in range(NSA_J)], axis=1).astype(BF16)
    qsa_ref[0:DH, :] = qt
    qwa_ref[0:DH, :] = qt
    er = lax.broadcasted_iota(jnp.int32, (DH, cols), 0)
    qwa_ref[DH:AUG, :] = jnp.where(er == 0, NEG_INF, 0.0).astype(BF16)

    ncmp = kc_ref.shape[0]
    n_idx = lax.broadcasted_iota(jnp.int32, (ncmp, 1), 0)
    rel_end = (n_idx * CMP_STRIDE + (CMP_LEN - 1)).astype(F32) - tl_f
    base = (i * tq).astype(F32)
    s = _dot(kc_ref[...], qt) + slope * rel_end
    s = jnp.where(rel_end <= base, s, NEG_INF)
    p = jnp.exp(s - jnp.max(s, axis=0, keepdims=True))
    p = p / jnp.sum(p, axis=0, keepdims=True)
    p = p * ((base + tl_f) >= (CMP_LEN - 1)).astype(F32)
    o_cmp = _dot(vct_ref[...], p.astype(BF16))

    psum = p[:, 0:tq]
    for j in range(1, NSA_J):
        psum = psum + p[:, j * tq:(j + 1) * tq]
    kk = lax.broadcasted_iota(jnp.int32, (LANE, ncmp), 0)
    nn = lax.broadcasted_iota(jnp.int32, (LANE, ncmp), 1)
    overlap = ((nn * CMP_STRIDE < (kk + 1) * SLC_BLOCK)
               & (nn * CMP_STRIDE + (CMP_LEN - 1) >= kk * SLC_BLOCK)
               & (kk < n_slc))
    imp = _dot_exact_lhs(jnp.where(overlap, 1.0, 0.0).astype(BF16), psum)[0:n_slc, :]
    blk = lax.broadcasted_iota(jnp.int32, (n_slc, tq), 0)
    blk_f = blk.astype(F32)
    cur = (i * tq + lax.broadcasted_iota(jnp.int32, (1, tq), 1)) // SLC_BLOCK
    val = jnp.where(blk > cur, NEG_INF, imp)
    val = jnp.where((blk == cur) | (blk == 0), FORCE_SCORE, val)
    sel = jnp.zeros((n_slc, tq), F32)
    for _ in range(min(SLC_TOPK, n_slc)):
        mx = jnp.max(val, axis=0, keepdims=True)
        idx = jnp.min(jnp.where(val == mx, blk_f, float(n_slc)), axis=0, keepdims=True)
        hit = blk_f == idx
        sel = jnp.where(hit, 1.0, sel)
        val = jnp.where(hit, BELOW_ALL, val)
    pen = (sel - 1.0) * (-NEG_INF)
    pen = jnp.concatenate([pen] * NSA_J, axis=1)
    er2 = lax.broadcasted_iota(jnp.int32, (DH - n_slc, cols), 0) + n_slc
    tail = jnp.where(er2 == COL_BLK, slope * SLC_BLOCK, jnp.where(er2 == COL_OFF, slope, 0.0))
    qsa_ref[DH:AUG, :] = jnp.concatenate([pen, tail], axis=0).astype(BF16)

    hw = 4 * LANE
    nhalf = cols // hw

    def slc_update(sc, c, state):
        m_run, acc = state
        m_new = jnp.maximum(m_run, jnp.max(sc, axis=0, keepdims=True))
        alpha = jnp.exp(m_run - m_new)
        prb = jnp.exp(sc - m_new).astype(BF16)
        return m_new, alpha * acc + _dot(vst_ref[c], prb)

    def slc_scores(c, h):
        start = pl.multiple_of(c * SLC_KCHUNK, SLC_KCHUNK)
        return _dot(ksa_ref[pl.ds(start, SLC_KCHUNK), :], qsa_ref[:, h * hw:(h + 1) * hw])

    def slc_step(c, carry):
        scs, states = carry
        nxt = tuple(slc_scores(c + 1, h) for h in range(nhalf))
        return nxt, tuple(slc_update(scs[h], c, states[h]) for h in range(nhalf))

    diag = (i * tq) // SLC_KCHUNK
    state0 = (jnp.full((1, hw), NEG_INF, F32), jnp.zeros((VT_ROWS, hw), F32))
    init = (tuple(slc_scores(0, h) for h in range(nhalf)), (state0,) * nhalf)
    scs, states = lax.fori_loop(0, diag, slc_step, init)
    krow = diag * SLC_KCHUNK + lax.broadcasted_iota(jnp.int32, (SLC_KCHUNK, 1), 0)
    o_halves = []
    for h in range(nhalf):
        causal = krow <= i * tq + tl[:, h * hw:(h + 1) * hw]
        _, acc_fin = slc_update(jnp.where(causal, scs[h], NEG_INF), diag, states[h])
        o_halves.append(acc_fin[0:DH] / acc_fin[DH:DH + 1])
    o_slc = jnp.concatenate(o_halves, axis=1)

    wstart = pl.multiple_of(i * tq, tq)
    sw = _dot(kwa_ref[pl.ds(wstart, WIN_SPAN), :], qwa_ref[...]) + bw_ref[...]
    pwb = jnp.exp(sw - jnp.max(sw, axis=0, keepdims=True)).astype(BF16)
    wblk = i * (tq // kb_rows)
    vwin = jnp.concatenate([vwt_ref[wblk + kb] for kb in range(WIN_SPAN // kb_rows)], axis=1)
    o_aug = _dot(vwin, pwb)
    o_win = o_aug[0:DH] / o_aug[DH:DH + 1]

    gt = jax.nn.sigmoid(gate_ref[...].astype(F32)).T

    def gate_row(branch):
        return jnp.concatenate([gt[3 * j + branch:3 * j + branch + 1, :] for j in range(NSA_J)], axis=1)

    out_t = gate_row(0) * o_cmp + gate_row(1) * o_slc + gate_row(2) * o_win
    o_ref[...] = jnp.concatenate([out_t[:, j * tq:(j + 1) * tq].T for j in range(NSA_J)], axis=1).astype(o_ref.dtype)


def _nsa(proj, cmp, batch, seq):
    t = proj.shape[0]
    nq = seq // NSA_TQ
    qw = NSA_J * DH
    kv0 = OFF_KV // DH

    def kv_spec(which):
        return pl.BlockSpec((seq, DH), lambda b, g, i: (b, kv0 + 2 * which + g))

    return pl.pallas_call(
        functools.partial(_nsa_body, seq=seq),
        grid=(batch, NSA_GROUPS, nq),
        in_specs=[
            pl.BlockSpec((NSA_TQ, qw), lambda b, g, i: (b * nq + i, g)),
            pl.BlockSpec((NSA_TQ, LANE), lambda b, g, i: (b * nq + i, OFF_NG // LANE + g)),
            pl.BlockSpec((None, None, None, seq // CMP_STRIDE, DH), lambda b, g, i: (b, 0, g, 0, 0)),
            pl.BlockSpec((None, None, None, seq // CMP_STRIDE, DH), lambda b, g, i: (b, 1, g, 0, 0)),
            kv_spec(2), kv_spec(3), kv_spec(4), kv_spec(5),
        ],
        out_specs=pl.BlockSpec((NSA_TQ, qw), lambda b, g, i: (b * nq + i, g)),
        out_shape=jax.ShapeDtypeStruct((t, NSA_Q), BF16),
        scratch_shapes=[
            pltpu.VMEM((seq, AUG), BF16),
            pltpu.VMEM((seq + WINDOW, AUG), BF16),
            pltpu.VMEM((seq // KEY_BLK, VT_ROWS, KEY_BLK), BF16),
            pltpu.VMEM((seq // KEY_BLK + WIN_PAD_BLOCKS, VT_ROWS, KEY_BLK), BF16),
            pltpu.VMEM((DH, seq // CMP_STRIDE), BF16),
            pltpu.VMEM((WIN_SPAN, NSA_COLS), F32),
            pltpu.VMEM((AUG, NSA_COLS), BF16),
            pltpu.VMEM((AUG, NSA_COLS), BF16),
        ],
        compiler_params=_params(("parallel", "arbitrary", "arbitrary")),
        name="nsa_attn",
    )(proj, proj, cmp, cmp, proj, proj, proj, proj)


SSD_W = SSD_HG * SSD_P


def _ssd_group(g, conv, a_cum, a_cum_t, dt, z_ref, dskip_ref, nw_ref, state_ref, o_ref, causal):
    lc = CHUNK
    ch = slice(g * SSD_W, (g + 1) * SSD_W)
    xs = conv[:, ch]
    bm = conv[:, SSD_INNER + g * SSD_N:SSD_INNER + (g + 1) * SSD_N]
    cm = conv[:, SSD_INNER + SSD_BC + g * SSD_N:SSD_INNER + SSD_BC + (g + 1) * SSD_N]
    eh = lax.broadcasted_iota(jnp.int32, (LANE, SSD_W), 0)
    ec = lax.broadcasted_iota(jnp.int32, (LANE, SSD_W), 1) // SSD_P + g * SSD_HG
    spread = jnp.where(eh == ec, 1.0, 0.0).astype(BF16)
    dt_x = _dot_exact_rhs(dt, spread)
    acum_x = _dot_exact_rhs(a_cum, spread)
    last_x = acum_x[lc - 1:lc, :]

    xdt = xs * dt_x
    xdt_bf = xdt.astype(BF16)
    cm_bf = cm.astype(BF16)
    cb = _dot_nt(cm_bf, bm.astype(BF16))

    y_heads = []
    for h in range(SSD_HG):
        col = acum_x[:, h * SSD_P:h * SSD_P + 1]
        rowv = a_cum_t[g * SSD_HG + h:g * SSD_HG + h + 1, :]
        decay = jnp.exp(jnp.where(causal, col - rowv, -jnp.inf))
        y_heads.append(_dot((cb * decay).astype(BF16), xdt_bf[:, h * SSD_P:(h + 1) * SSD_P]))
    y_diag = jnp.concatenate(y_heads, axis=-1)

    state = state_ref[:, ch]
    y_off = _dot(cm_bf, state.astype(BF16)) * jnp.exp(acum_x)
    to_end = jnp.exp(last_x - acum_x)
    chunk_state = _dot(bm.T.astype(BF16), (xdt * to_end).astype(BF16))
    state_ref[:, ch] = state * jnp.exp(last_x) + chunk_state

    y = y_diag + y_off + xs * dskip_ref[:, ch]
    y = y * jax.nn.silu(z_ref[:, ch].astype(F32))
    y = y * lax.rsqrt(jnp.mean(y * y, axis=-1, keepdims=True) + EPS)
    o_ref[:, ch] = (y * nw_ref[:, ch]).astype(o_ref.dtype)


def _ssd_body(xbc_ref, z_ref, dt_ref, cw_ref, cb_ref, dtb_ref, alog_ref, dskip_ref, nw_ref, o_ref,
              state_ref, prev_ref, shift_ref):
    first = pl.program_id(1) == 0
    lc = CHUNK
    taps = SSD_CONV - 1

    @pl.when(first)
    def _():
        state_ref[...] = jnp.zeros_like(state_ref)
        prev_ref[...] = jnp.zeros_like(prev_ref)
        r = lax.broadcasted_iota(jnp.int32, (taps * lc, 2 * lc), 0)
        c = lax.broadcasted_iota(jnp.int32, (taps * lc, 2 * lc), 1)
        k = r // lc
        shift_ref[...] = jnp.where(c == lc + (r - k * lc) - (k + 1), 1.0, 0.0).astype(BF16)

    cur = xbc_ref[...]
    shifted = _dot(shift_ref[...], jnp.concatenate([prev_ref[...], cur], axis=0))
    prev_ref[...] = cur
    conv = cb_ref[...] + cw_ref[taps:taps + 1, :] * cur.astype(F32)
    for back in range(1, SSD_CONV):
        conv = conv + cw_ref[taps - back:taps - back + 1, :] * shifted[(back - 1) * lc:back * lc]
    conv = jax.nn.silu(conv)

    dt_all = jax.nn.softplus(dt_ref[...].astype(F32) + dtb_ref[...])
    a_all = dt_all * -jnp.exp(alog_ref[...])
    li = lax.broadcasted_iota(jnp.int32, (lc, lc), 0)
    si = lax.broadcasted_iota(jnp.int32, (lc, lc), 1)
    causal = li >= si
    a_cum_all = _dot_exact_lhs(jnp.where(causal, 1.0, 0.0).astype(BF16), a_all)
    a_cum_t = a_cum_all.T
    for g in range(SSD_GROUPS):
        _ssd_group(g, conv, a_cum_all, a_cum_t, dt_all, z_ref, dskip_ref, nw_ref, state_ref, o_ref, causal)


def _ssd(proj, lw, layer, batch, seq):
    t = proj.shape[0]
    nc = seq // CHUNK

    def rows(b, c):
        return pl.multiple_of((b * nc + c) * CHUNK, CHUNK)

    def window(width, offset):
        return pl.BlockSpec((pl.Element(CHUNK), pl.Element(width)), lambda b, c: (rows(b, c), offset))

    return pl.pallas_call(
        _ssd_body,
        grid=(batch, nc),
        in_specs=[
            window(SSD_CONV_DIM, OFF_XBC),
            window(SSD_INNER, OFF_Z),
            window(LANE, OFF_DT),
            pl.BlockSpec((None, SSD_CONV, SSD_CONV_DIM), lambda b, c: (layer, 0, 0)),
            pl.BlockSpec((None, 1, SSD_CONV_DIM), lambda b, c: (layer, 0, 0)),
            pl.BlockSpec((None, 1, LANE), lambda b, c: (layer, 0, 0)),
            pl.BlockSpec((None, 1, LANE), lambda b, c: (layer, 0, 0)),
            pl.BlockSpec((None, 1, SSD_INNER), lambda b, c: (layer, 0, 0)),
            pl.BlockSpec((None, 1, SSD_INNER), lambda b, c: (layer, 0, 0)),
        ],
        out_specs=pl.BlockSpec((CHUNK, SSD_INNER), lambda b, c: (b * nc + c, 0)),
        out_shape=jax.ShapeDtypeStruct((t, SSD_INNER), BF16),
        scratch_shapes=[
            pltpu.VMEM((SSD_N, SSD_INNER), F32),
            pltpu.VMEM((CHUNK, SSD_CONV_DIM), BF16),
            pltpu.VMEM(((SSD_CONV - 1) * CHUNK, 2 * CHUNK), BF16),
        ],
        compiler_params=_params(("parallel", "arbitrary")),
        name="ssd",
    )(proj, proj, proj, lw["conv_w"], lw["conv_b"], lw["dt_bias"], lw["a_log"], lw["d_skip"], lw["ssd_norm"])


def _ret_log_gamma(h):
    return float(np.log1p(-np.exp2(-5.0 - h)))


def _ret_body(q_ref, k_ref, v0_ref, v1_ref, g0_ref, g1_ref, nw_ref, o_ref, state_ref):
    lc = CHUNK
    half = RET_HEADS // 2
    v_refs = (v0_ref, v1_ref)
    g_refs = (g0_ref, g1_ref)

    @pl.when(pl.program_id(1) == 0)
    def _():
        state_ref[...] = jnp.zeros_like(state_ref)

    li = lax.broadcasted_iota(jnp.int32, (lc, lc), 0)
    si = lax.broadcasted_iota(jnp.int32, (lc, lc), 1)
    diff = (li - si).astype(F32)
    idx = lax.broadcasted_iota(jnp.int32, (lc, 1), 0).astype(F32)

    for h in range(RET_HEADS):
        log_g = _ret_log_gamma(h)
        dmat = jnp.where(diff >= 0, jnp.exp(diff * log_g), 0.0)
        k_dec = jnp.exp((lc - 1.0 - idx) * log_g)
        q_dec = jnp.exp((idx + 1.0) * log_g)
        q = q_ref[:, h * RET_DK:(h + 1) * RET_DK].astype(F32) * (RET_DK ** -0.5)
        k = k_ref[:, h * RET_DK:(h + 1) * RET_DK].astype(F32)
        vsl = slice((h % half) * RET_DV, (h % half + 1) * RET_DV)
        v = v_refs[h // half][:, vsl]
        scores = _dot_nt(q.astype(BF16), k.astype(BF16)) * dmat
        state = state_ref[h]
        y = _dot(scores.astype(BF16), v) + _dot((q * q_dec).astype(BF16), state.astype(BF16))
        chunk_kv = _dot((k * k_dec).T.astype(BF16), v)
        state_ref[h] = state * float(np.exp(lc * log_g)) + chunk_kv

        mu = jnp.mean(y, axis=-1, keepdims=True)
        yc = y - mu
        var = jnp.mean(yc * yc, axis=-1, keepdims=True)
        y = yc * lax.rsqrt(var + EPS) * nw_ref[h]
        gate = jax.nn.silu(g_refs[h // half][:, vsl].astype(F32))
        o_ref[:, h * RET_DV:(h + 1) * RET_DV] = (gate * y).astype(o_ref.dtype)


def _retention(proj, nw, layer, batch, seq):
    t = proj.shape[0]
    nc = seq // CHUNK
    hw = RET_V // 2
    q0 = OFF_RQK // RET_QK
    v0 = OFF_RV // hw
    g0 = OFF_RG // hw
    return pl.pallas_call(
        _ret_body,
        grid=(batch, nc),
        in_specs=[
            pl.BlockSpec((CHUNK, RET_QK), lambda b, c: (b * nc + c, q0)),
            pl.BlockSpec((CHUNK, RET_QK), lambda b, c: (b * nc + c, q0 + 1)),
            pl.BlockSpec((CHUNK, hw), lambda b, c: (b * nc + c, v0)),
            pl.BlockSpec((CHUNK, hw), lambda b, c: (b * nc + c, v0 + 1)),
            pl.BlockSpec((CHUNK, hw), lambda b, c: (b * nc + c, g0)),
            pl.BlockSpec((CHUNK, hw), lambda b, c: (b * nc + c, g0 + 1)),
            pl.BlockSpec((None, RET_HEADS, 1, RET_DV), lambda b, c: (layer, 0, 0, 0)),
        ],
        out_specs=pl.BlockSpec((CHUNK, RET_V), lambda b, c: (b * nc + c, 0)),
        out_shape=jax.ShapeDtypeStruct((t, RET_V), BF16),
        scratch_shapes=[pltpu.VMEM((RET_HEADS, RET_DK, RET_DV), F32)],
        compiler_params=_params(("parallel", "arbitrary")),
        name="retention",
    )(proj, proj, proj, proj, proj, proj, nw)


def _per_group_rows(seg, groups):
    depth, rows, d = seg.shape
    per = rows // groups
    seg = seg.reshape(depth, groups, per, d)
    seg = jnp.pad(seg, [(0, 0), (0, 0), (0, LANE - per), (0, 0)])
    return seg.reshape(depth, groups * LANE, d)


def _small_w_t(w_t):
    o_ng = IN_SPLITS[0] + IN_SPLITS[1]
    o_dt = o_ng + IN_SPLITS[2] + IN_SPLITS[3] + IN_SPLITS[4]
    ng = _per_group_rows(w_t[:, o_ng:o_ng + IN_SPLITS[2], :], NSA_GROUPS)
    dt = w_t[:, o_dt:o_dt + IN_SPLITS[5], :]
    pad = jnp.zeros((w_t.shape[0], N_SMALL - ng.shape[1] - dt.shape[1], w_t.shape[2]), w_t.dtype)
    return jnp.concatenate([ng, dt, pad], axis=1)


def _pad_lanes(v):
    depth, heads = v.shape
    return jnp.pad(v, [(0, 0), (0, LANE - heads)]).reshape(depth, 1, LANE)


def _layer(x, lw, layer, batch, seq):
    u = _rms(x, lw["norm_mix"], layer, BF16)
    proj, xc = _inproj(u, lw["w_t"], lw["w_small_t"], layer)
    cmp = _compress(xc, lw["cmp_pe"][layer], lw["cmp_w1"][layer], lw["cmp_w2"][layer], batch)

    y_nsa = _nsa(proj, cmp, batch, seq)
    y_ssd = _ssd(proj, lw, layer, batch, seq)
    y_ret = _retention(proj, lw["ret_norm"], layer, batch, seq)

    merged = _merge(y_nsa, y_ssd, y_ret, proj, lw["p_nsa"], lw["p_ssd"], lw["p_ret"], layer)
    x = _mm_res(merged, lw["w_out"], x, layer, 512)
    f = _rms(x, lw["norm_ffn"], layer, BF16)
    hidden = _ffn_up(f, lw["w_gate"], lw["w_up"], layer)
    return _mm_res(hidden, lw["w_down"], x, layer, 256)


def kernel(x, norm_mix, w_in, cmp_k_pe, cmp_k_w1, cmp_k_w2, cmp_v_pe, cmp_v_w1, cmp_v_w2, conv_w, conv_b, dt_bias, a_log, d_skip, ssd_norm, ret_norm, p_nsa, p_ssd, p_ret, w_out, norm_ffn, w_gate, w_up, w_down, norm_final):
    batch, seq, d = x.shape
    depth = w_in.shape[0]
    w_t = jnp.swapaxes(w_in, 1, 2)
    pe = jnp.stack([cmp_k_pe, cmp_v_pe], axis=1).reshape(depth, 2, 1, CMP_LEN * DH)
    lw = {
        "norm_mix": norm_mix[:, None, :],
        "w_t": w_t,
        "w_small_t": _small_w_t(w_t),
        "cmp_pe": jnp.pad(pe, [(0, 0), (0, 0), (0, 7), (0, 0)]).astype(BF16),
        "cmp_w1": jnp.stack([cmp_k_w1, cmp_v_w1], axis=1).astype(BF16),
        "cmp_w2": jnp.stack([cmp_k_w2, cmp_v_w2], axis=1).astype(BF16),
        "conv_w": conv_w,
        "conv_b": conv_b[:, None, :],
        "dt_bias": _pad_lanes(dt_bias),
        "a_log": _pad_lanes(a_log),
        "d_skip": jnp.repeat(d_skip, SSD_P, axis=-1)[:, None, :],
        "ssd_norm": ssd_norm[:, None, :],
        "ret_norm": ret_norm[:, :, None, :],
        "p_nsa": p_nsa, "p_ssd": p_ssd, "p_ret": p_ret, "w_out": w_out,
        "norm_ffn": norm_ffn[:, None, :],
        "w_gate": w_gate, "w_up": w_up, "w_down": w_down,
    }
    h = x.reshape(batch * seq, d)
    for layer in range(depth):
        h = _layer(h, lw, layer, batch, seq)
    return _rms(h, norm_final[None, None, :], 0, F32).reshape(batch, seq, d)
```

```python
import functools

import jax
import jax.numpy as jnp
import numpy as np
from jax import lax
from jax.experimental import pallas as pl
from jax.experimental.pallas import tpu as pltpu

F32 = jnp.float32
BF16 = jnp.bfloat16

D_MODEL = 2048
NSA_HEADS = 8
NSA_GROUPS = 2
NSA_J = NSA_HEADS // NSA_GROUPS
DH = 128
CMP_LEN = 32
CMP_STRIDE = 16
SLC_BLOCK = 64
SLC_TOPK = 8
WINDOW = 512
SSD_INNER = D_MODEL
SSD_P = 64
SSD_HEADS = SSD_INNER // SSD_P
SSD_GROUPS = 4
SSD_HG = SSD_HEADS // SSD_GROUPS
SSD_N = 128
SSD_CONV = 4
CHUNK = 128
RET_HEADS = 4
RET_DK = 128
RET_DV = 256
D_FF = -(-8 * D_MODEL // (3 * 256)) * 256
NSA_Q = NSA_HEADS * DH
NSA_KV = NSA_GROUPS * DH
SSD_BC = SSD_GROUPS * SSD_N
SSD_CONV_DIM = SSD_INNER + 2 * SSD_BC
RET_QK = RET_HEADS * RET_DK
RET_V = RET_HEADS * RET_DV
IN_SPLITS = (NSA_Q, 6 * NSA_KV, 3 * NSA_HEADS, SSD_INNER, SSD_CONV_DIM, SSD_HEADS,
             2 * RET_QK, RET_V, RET_V, 3 * D_MODEL)
D_IN = sum(IN_SPLITS)
NEG_INF = -1e30
FORCE_SCORE = 1e9
BELOW_ALL = -3e38
EPS = 1e-6

LANE = 128

OFF_Q = 0
OFF_KV = OFF_Q + NSA_Q
OFF_Z = OFF_KV + 6 * NSA_KV
OFF_XBC = OFF_Z + SSD_INNER
OFF_RQK = OFF_XBC + SSD_CONV_DIM
OFF_RV = OFF_RQK + 2 * RET_QK
OFF_RG = OFF_RV + RET_V
OFF_MG = OFF_RG + RET_V
OFF_NG = OFF_MG + 3 * D_MODEL
OFF_DT = OFF_NG + NSA_GROUPS * LANE
N_USED = OFF_DT + LANE
PROJ_TN = 512
N_PACK = -(-N_USED // PROJ_TN) * PROJ_TN
N_MAIN_BLOCKS = OFF_NG // PROJ_TN
N_SMALL = N_PACK - OFF_NG

SHIFT_B = IN_SPLITS[2]
SHIFT_C = IN_SPLITS[2] + IN_SPLITS[5]
BLK_B = OFF_Z // PROJ_TN
BLK_C = OFF_RQK // PROJ_TN

ROW_TILE = 2048
VMEM_LIMIT = 56 * 1024 * 1024


def _dot(a, b):
    return jnp.dot(a, b, preferred_element_type=F32)


def _dot_nt(a, b):
    return lax.dot_general(a, b, (((1,), (1,)), ((), ())), preferred_element_type=F32)


def _split3(x):
    hi = x.astype(BF16)
    r1 = x - hi.astype(F32)
    mid = r1.astype(BF16)
    lo = (r1 - mid.astype(F32)).astype(BF16)
    return hi, mid, lo


def _dot_exact_rhs(x, m_bf16):
    hi, mid, lo = _split3(x)
    return _dot(hi, m_bf16) + _dot(mid, m_bf16) + _dot(lo, m_bf16)


def _dot_exact_lhs(m_bf16, x):
    hi, mid, lo = _split3(x)
    return _dot(m_bf16, hi) + _dot(m_bf16, mid) + _dot(m_bf16, lo)


def _params(sem):
    return pltpu.CompilerParams(dimension_semantics=sem, vmem_limit_bytes=VMEM_LIMIT)


def _rms_body(x_ref, nw_ref, o_ref):
    slab = 128

    def body(r, carry):
        sl = pl.ds(pl.multiple_of(r * slab, slab), slab)
        x = x_ref[sl, :]
        ms = jnp.mean(x * x, axis=-1, keepdims=True)
        o_ref[sl, :] = (x * lax.rsqrt(ms + EPS) * nw_ref[...]).astype(o_ref.dtype)
        return carry

    lax.fori_loop(0, x_ref.shape[0] // slab, body, 0)


def _rms(x, nw, layer, out_dtype):
    t, d = x.shape
    tm = 1024
    return pl.pallas_call(
        _rms_body,
        grid=(t // tm,),
        in_specs=[pl.BlockSpec((tm, d), lambda i: (i, 0)),
                  pl.BlockSpec((None, 1, d), lambda i: (layer, 0, 0))],
        out_specs=pl.BlockSpec((tm, d), lambda i: (i, 0)),
        out_shape=jax.ShapeDtypeStruct((t, d), out_dtype),
        compiler_params=_params(("parallel",)),
        name="rmsnorm",
    )(x, nw)


CMP_BLK = OFF_KV // PROJ_TN
CMP_STREAMS = 2 * NSA_GROUPS
CMP_CHUNK_W = CMP_STRIDE * DH


def _inproj_body(u_ref, w_ref, ws_ref, o_ref, xc_ref, stage_ref):
    j = pl.program_id(1)

    @pl.when(j < N_MAIN_BLOCKS)
    def _():
        acc = _dot_nt(u_ref[...], w_ref[0].astype(BF16))
        o_ref[...] = acc.astype(o_ref.dtype)

        @pl.when(j == CMP_BLK)
        def _():
            nchunk = stage_ref.shape[1] // CMP_STRIDE
            for s in range(CMP_STREAMS):
                stage_ref[s] = acc[:, s * DH:(s + 1) * DH]
            for s in range(CMP_STREAMS):
                for l in range(CMP_STRIDE):
                    lo = s * CMP_CHUNK_W + l * DH
                    rows = stage_ref[s, pl.ds(l, nchunk, stride=CMP_STRIDE), :]
                    xc_ref[:, lo:lo + DH] = rows.astype(xc_ref.dtype)

    @pl.when(j >= N_MAIN_BLOCKS)
    def _():
        o_ref[...] = _dot_nt(u_ref[...], ws_ref[...].astype(BF16)).astype(o_ref.dtype)


def _inproj(u, w_t, w_small_t, layer):
    t, d = u.shape
    tm, tn = ROW_TILE, PROJ_TN
    assert OFF_KV == CMP_BLK * tn and CMP_STREAMS * DH == tn
    last_main = N_MAIN_BLOCKS - 1

    def w_index(i, j):
        jc = jnp.minimum(j, last_main)
        shift = jnp.where(jc >= BLK_C, SHIFT_C, jnp.where(jc >= BLK_B, SHIFT_B, 0))
        return (layer, pl.multiple_of(jc * tn + shift, 8), 0)

    return pl.pallas_call(
        _inproj_body,
        grid=(t // tm, N_PACK // tn),
        in_specs=[
            pl.BlockSpec((tm, d), lambda i, j: (i, 0)),
            pl.BlockSpec((pl.Element(1), pl.Element(tn), pl.Element(d)), w_index),
            pl.BlockSpec((None, tn, d), lambda i, j: (layer, jnp.maximum(j - N_MAIN_BLOCKS, 0), 0)),
        ],
        out_specs=[
            pl.BlockSpec((tm, tn), lambda i, j: (i, j)),
            pl.BlockSpec((tm // CMP_STRIDE, CMP_STREAMS * CMP_CHUNK_W), lambda i, j: (i, 0)),
        ],
        out_shape=[
            jax.ShapeDtypeStruct((t, N_PACK), BF16),
            jax.ShapeDtypeStruct((t // CMP_STRIDE, CMP_STREAMS * CMP_CHUNK_W), BF16),
        ],
        scratch_shapes=[pltpu.VMEM((CMP_STREAMS, tm, DH), F32)],
        compiler_params=_params(("parallel", "arbitrary")),
        name="inproj",
    )(u, w_t, w_small_t)


def _merge_body(yn_ref, ys_ref, yr_ref, gn_ref, gs_ref, gr_ref, pn_ref, ps_ref, pr_ref, o_ref):
    def gate(g_ref):
        return jax.nn.sigmoid(g_ref[...].astype(F32))

    acc = gate(gn_ref) * _dot(yn_ref[...], pn_ref[...].astype(BF16))
    acc = acc + gate(gs_ref) * _dot(ys_ref[...], ps_ref[...].astype(BF16))
    acc = acc + gate(gr_ref) * _dot(yr_ref[...], pr_ref[...].astype(BF16))
    o_ref[...] = acc.astype(o_ref.dtype)


def _merge(y_nsa, y_ssd, y_ret, proj, p_nsa, p_ssd, p_ret, layer):
    t = y_nsa.shape[0]
    tm, tn = 1024, 512
    g0 = OFF_MG // tn
    gstep = D_MODEL // tn
    return pl.pallas_call(
        _merge_body,
        grid=(t // tm, D_MODEL // tn),
        in_specs=[
            pl.BlockSpec((tm, NSA_Q), lambda i, j: (i, 0)),
            pl.BlockSpec((tm, SSD_INNER), lambda i, j: (i, 0)),
            pl.BlockSpec((tm, RET_V), lambda i, j: (i, 0)),
            pl.BlockSpec((tm, tn), lambda i, j: (i, g0 + j)),
            pl.BlockSpec((tm, tn), lambda i, j: (i, g0 + gstep + j)),
            pl.BlockSpec((tm, tn), lambda i, j: (i, g0 + 2 * gstep + j)),
            pl.BlockSpec((None, NSA_Q, tn), lambda i, j: (layer, 0, j)),
            pl.BlockSpec((None, SSD_INNER, tn), lambda i, j: (layer, 0, j)),
            pl.BlockSpec((None, RET_V, tn), lambda i, j: (layer, 0, j)),
        ],
        out_specs=pl.BlockSpec((tm, tn), lambda i, j: (i, j)),
        out_shape=jax.ShapeDtypeStruct((t, D_MODEL), BF16),
        compiler_params=_params(("parallel", "arbitrary")),
        name="merge",
    )(y_nsa, y_ssd, y_ret, proj, proj, proj, p_nsa, p_ssd, p_ret)


def _mm_res_body(a_ref, w_ref, r_ref, o_ref):
    o_ref[...] = r_ref[...] + _dot(a_ref[...], w_ref[...].astype(BF16))


def _mm_res(a, w, res, layer, tn):
    t, k = a.shape
    n = w.shape[2]
    tm = ROW_TILE if k <= D_MODEL else ROW_TILE // 2
    return pl.pallas_call(
        _mm_res_body,
        grid=(t // tm, n // tn),
        in_specs=[
            pl.BlockSpec((tm, k), lambda i, j: (i, 0)),
            pl.BlockSpec((None, k, tn), lambda i, j: (layer, 0, j)),
            pl.BlockSpec((tm, tn), lambda i, j: (i, j)),
        ],
        out_specs=pl.BlockSpec((tm, tn), lambda i, j: (i, j)),
        out_shape=jax.ShapeDtypeStruct((t, n), F32),
        compiler_params=_params(("parallel", "arbitrary")),
        name="mm_res",
    )(a, w, res)


def _mm_res_norm_body(a_ref, w_ref, r_ref, nw_ref, o_ref, f_ref, row_ref):
    j = pl.program_id(1)
    nj = pl.num_programs(1)
    tn = o_ref.shape[1]
    acc = r_ref[...] + _dot(a_ref[...], w_ref[...].astype(BF16))
    o_ref[...] = acc
    for jj in range(row_ref.shape[1] // tn):
        @pl.when(j == jj)
        def _(jj=jj):
            row_ref[:, jj * tn:(jj + 1) * tn] = acc

    @pl.when(j == nj - 1)
    def _():
        _rms_body(row_ref, nw_ref, f_ref)


def _mm_res_norm(a, w, res, nw, layer, tn):
    t, k = a.shape
    n = w.shape[2]
    tm = ROW_TILE // 2
    return pl.pallas_call(
        _mm_res_norm_body,
        grid=(t // tm, n // tn),
        in_specs=[
            pl.BlockSpec((tm, k), lambda i, j: (i, 0)),
            pl.BlockSpec((None, k, tn), lambda i, j: (layer, 0, j)),
            pl.BlockSpec((tm, tn), lambda i, j: (i, j)),
            pl.BlockSpec((None, 1, n), lambda i, j: (layer, 0, 0)),
        ],
        out_specs=[
            pl.BlockSpec((tm, tn), lambda i, j: (i, j)),
            pl.BlockSpec((tm, n), lambda i, j: (i, 0)),
        ],
        out_shape=[jax.ShapeDtypeStruct((t, n), F32), jax.ShapeDtypeStruct((t, n), BF16)],
        scratch_shapes=[pltpu.VMEM((tm, n), F32)],
        compiler_params=_params(("parallel", "arbitrary")),
        name="mm_res_norm",
    )(a, w, res, nw)


def _ffn_up_body(f_ref, wg_ref, wu_ref, o_ref):
    f = f_ref[...]
    gate = _dot(f, wg_ref[...].astype(BF16))
    up = _dot(f, wu_ref[...].astype(BF16))
    o_ref[...] = (jax.nn.silu(gate) * up).astype(o_ref.dtype)


def _ffn_up(f, wg, wu, layer):
    t, d = f.shape
    n = wg.shape[2]
    tm, tn = ROW_TILE, 512
    return pl.pallas_call(
        _ffn_up_body,
        grid=(t // tm, n // tn),
        in_specs=[
            pl.BlockSpec((tm, d), lambda i, j: (i, 0)),
            pl.BlockSpec((None, d, tn), lambda i, j: (layer, 0, j)),
            pl.BlockSpec((None, d, tn), lambda i, j: (layer, 0, j)),
        ],
        out_specs=pl.BlockSpec((tm, tn), lambda i, j: (i, j)),
        out_shape=jax.ShapeDtypeStruct((t, n), BF16),
        compiler_params=_params(("parallel", "arbitrary")),
        name="ffn_up",
    )(f, wg, wu)


def _compress_body(x_ref, pe_ref, w1_ref, w2_ref, o_ref):
    half = (CMP_LEN // 2) * DH
    x = x_ref[...]
    a = _dot(x, w1_ref[:half, :])
    b = _dot(x, w1_ref[half:, :])
    bias = _dot(pe_ref[...], w1_ref[...])[0:1, :]
    nblk = x.shape[0]
    h = a + pltpu.roll(b, nblk - 1, 0) + bias
    h = jax.nn.gelu(h)
    o_ref[...] = _dot(h.astype(BF16), w2_ref[...]).astype(o_ref.dtype)


def _compress(xc, pe, w1, w2, batch):
    b, g = batch, NSA_GROUPS
    nchunk = xc.shape[0] // batch
    width = CMP_CHUNK_W
    return pl.pallas_call(
        _compress_body,
        grid=(b, 2, g),
        in_specs=[
            pl.BlockSpec((nchunk, width), lambda bi, m, gi: (bi, m * NSA_GROUPS + gi)),
            pl.BlockSpec((None, 8, CMP_LEN * DH), lambda bi, m, gi: (m, 0, 0)),
            pl.BlockSpec((None, CMP_LEN * DH, DH), lambda bi, m, gi: (m, 0, 0)),
            pl.BlockSpec((None, DH, DH), lambda bi, m, gi: (m, 0, 0)),
        ],
        out_specs=pl.BlockSpec((None, None, None, nchunk, DH), lambda bi, m, gi: (bi, m, gi, 0, 0)),
        out_shape=jax.ShapeDtypeStruct((b, 2, g, nchunk, DH), BF16),
        compiler_params=_params(("parallel", "arbitrary", "arbitrary")),
        name="nsa_compress",
    )(xc, pe, w1, w2)


NSA_TQ = 256
KEY_BLK = 256
VT_ROWS = DH + 16
NSA_COLS = NSA_J * NSA_TQ
SLC_KCHUNK = 256
WIN_SPAN = WINDOW + NSA_TQ
WIN_PAD_BLOCKS = WINDOW // KEY_BLK
AUG = 2 * DH
COL_BLK = SLC_BLOCK // 2
COL_OFF = COL_BLK + 1


def _nsa_body(q_ref, gate_ref, kc_ref, vc_ref, ks_ref, vs_ref, kw_ref, vw_ref, o_ref,
              ksa_ref, kwa_ref, vst_ref, vwt_ref, vct_ref, bw_ref, qsa_ref, qwa_ref, *, seq):
    g = pl.program_id(1)
    i = pl.program_id(2)
    tq = NSA_TQ
    cols = NSA_COLS
    n_slc = seq // SLC_BLOCK
    kb_rows = KEY_BLK
    nkb = seq // kb_rows

    cc = lax.broadcasted_iota(jnp.int32, (1, cols), 1)
    head = cc // tq
    tl = cc - head * tq
    slope_g = jnp.where(g == 0, 0.5, 0.5 ** (NSA_J + 1)).astype(F32)
    slope = slope_g * jnp.where(head == 0, 1.0, jnp.where(head == 1, 0.5, jnp.where(head == 2, 0.25, 0.125)))
    tl_f = tl.astype(F32)

    @pl.when(i == 0)
    def _init():
        def blk_body(kb, carry):
            r0 = pl.multiple_of(kb * kb_rows, kb_rows)
            rows = pl.ds(r0, kb_rows)
            key = r0 + lax.broadcasted_iota(jnp.int32, (kb_rows, DH), 0)
            col = lax.broadcasted_iota(jnp.int32, (kb_rows, DH), 1)
            blk = key // SLC_BLOCK
            off = key - blk * SLC_BLOCK
            extra = jnp.where(col == blk, 1.0, 0.0)
            extra = jnp.where(col == COL_BLK, blk.astype(F32), extra)
            extra = jnp.where(col == COL_OFF, off.astype(F32), extra)
            ksa_ref[rows, 0:DH] = ks_ref[rows, :]
            ksa_ref[rows, DH:AUG] = extra.astype(BF16)
            prow = pl.ds(pl.multiple_of(r0 + WINDOW, kb_rows), kb_rows)
            kwa_ref[prow, 0:DH] = kw_ref[rows, :]
            kwa_ref[prow, DH:AUG] = jnp.zeros((kb_rows, DH), BF16)
            orow = lax.broadcasted_iota(jnp.int32, (VT_ROWS - DH, kb_rows), 0)
            ones_row = jnp.where(orow == 0, 1.0, 0.0).astype(BF16)
            vst_ref[kb, 0:DH, :] = vs_ref[rows, :].astype(F32).T.astype(BF16)
            vst_ref[kb, DH:VT_ROWS, :] = ones_row
            vwt_ref[kb + WIN_PAD_BLOCKS, 0:DH, :] = vw_ref[rows, :].astype(F32).T.astype(BF16)
            vwt_ref[kb + WIN_PAD_BLOCKS, DH:VT_ROWS, :] = ones_row
            return carry

        lax.fori_loop(0, nkb, blk_body, 0)
        pcol = lax.broadcasted_iota(jnp.int32, (WINDOW, DH), 1)
        kwa_ref[0:WINDOW, 0:DH] = jnp.zeros((WINDOW, DH), BF16)
        kwa_ref[0:WINDOW, DH:AUG] = jnp.where(pcol == 0, 1.0, 0.0).astype(BF16)
        vwt_ref[0:WIN_PAD_BLOCKS] = jnp.zeros((WIN_PAD_BLOCKS, VT_ROWS, kb_rows), BF16)
        vct_ref[...] = vc_ref[...].astype(F32).T.astype(BF16)
        wk = lax.broadcasted_iota(jnp.int32, (WIN_SPAN, cols), 0)
        d = (WINDOW + tl - wk).astype(F32)
        bw_ref[...] = jnp.where((d >= 0) & (d < WINDOW), -slope * d, NEG_INF)

    q = q_ref[...].astype(F32) * (DH ** -0.5)
    qt = jnp.concatenate([q[:, j * DH:(j + 1) * DH].T for j in range(NSA_J)], axis=1).astype(BF16)
    qsa_ref[0:DH, :] = qt
    qwa_ref[0:DH, :] = qt
    er = lax.broadcasted_iota(jnp.int32, (DH, cols), 0)
    qwa_ref[DH:AUG, :] = jnp.where(er == 0, NEG_INF, 0.0).astype(BF16)

    ncmp = kc_ref.shape[0]
    n_idx = lax.broadcasted_iota(jnp.int32, (ncmp, 1), 0)
    rel_end = (n_idx * CMP_STRIDE + (CMP_LEN - 1)).astype(F32) - tl_f
    base = (i * tq).astype(F32)
    s = _dot(kc_ref[...], qt) + slope * rel_end
    s = jnp.where(rel_end <= base, s, NEG_INF)
    p = jnp.exp(s - jnp.max(s, axis=0, keepdims=True))
    p = p / jnp.sum(p, axis=0, keepdims=True)
    p = p * ((base + tl_f) >= (CMP_LEN - 1)).astype(F32)
    o_cmp = _dot(vct_ref[...], p.astype(BF16))

    psum = p[:, 0:tq]
    for j in range(1, NSA_J):
        psum = psum + p[:, j * tq:(j + 1) * tq]
    kk = lax.broadcasted_iota(jnp.int32, (LANE, ncmp), 0)
    nn = lax.broadcasted_iota(jnp.int32, (LANE, ncmp), 1)
    overlap = ((nn * CMP_STRIDE < (kk + 1) * SLC_BLOCK)
               & (nn * CMP_STRIDE + (CMP_LEN - 1) >= kk * SLC_BLOCK)
               & (kk < n_slc))
    imp = _dot_exact_lhs(jnp.where(overlap, 1.0, 0.0).astype(BF16), psum)[0:n_slc, :]
    blk = lax.broadcasted_iota(jnp.int32, (n_slc, tq), 0)
    blk_f = blk.astype(F32)
    cur = (i * tq + lax.broadcasted_iota(jnp.int32, (1, tq), 1)) // SLC_BLOCK
    val = jnp.where(blk > cur, NEG_INF, imp)
    val = jnp.where((blk == cur) | (blk == 0), FORCE_SCORE, val)
    sel = jnp.zeros((n_slc, tq), F32)
    for _ in range(min(SLC_TOPK, n_slc)):
        mx = jnp.max(val, axis=0, keepdims=True)
        idx = jnp.min(jnp.where(val == mx, blk_f, float(n_slc)), axis=0, keepdims=True)
        hit = blk_f == idx
        sel = jnp.where(hit, 1.0, sel)
        val = jnp.where(hit, BELOW_ALL, val)
    pen = (sel - 1.0) * (-NEG_INF)
    pen = jnp.concatenate([pen] * NSA_J, axis=1)
    er2 = lax.broadcasted_iota(jnp.int32, (DH - n_slc, cols), 0) + n_slc
    tail = jnp.where(er2 == COL_BLK, slope * SLC_BLOCK, jnp.where(er2 == COL_OFF, slope, 0.0))
    qsa_ref[DH:AUG, :] = jnp.concatenate([pen, tail], axis=0).astype(BF16)

    hw = 4 * LANE
    nhalf = cols // hw

    def slc_update(sc, c, state):
        m_run, acc = state
        m_new = jnp.maximum(m_run, jnp.max(sc, axis=0, keepdims=True))
        alpha = jnp.exp(m_run - m_new)
        prb = jnp.exp(sc - m_new).astype(BF16)
        return m_new, alpha * acc + _dot(vst_ref[c], prb)

    def slc_scores(c, h):
        start = pl.multiple_of(c * SLC_KCHUNK, SLC_KCHUNK)
        return _dot(ksa_ref[pl.ds(start, SLC_KCHUNK), :], qsa_ref[:, h * hw:(h + 1) * hw])

    def slc_step(c, carry):
        scs, states = carry
        nxt = tuple(slc_scores(c + 1, h) for h in range(nhalf))
        return nxt, tuple(slc_update(scs[h], c, states[h]) for h in range(nhalf))

    diag = (i * tq) // SLC_KCHUNK
    state0 = (jnp.full((1, hw), NEG_INF, F32), jnp.zeros((VT_ROWS, hw), F32))
    init = (tuple(slc_scores(0, h) for h in range(nhalf)), (state0,) * nhalf)
    scs, states = lax.fori_loop(0, diag, slc_step, init)
    krow = diag * SLC_KCHUNK + lax.broadcasted_iota(jnp.int32, (SLC_KCHUNK, 1), 0)
    o_halves = []
    for h in range(nhalf):
        causal = krow <= i * tq + tl[:, h * hw:(h + 1) * hw]
        _, acc_fin = slc_update(jnp.where(causal, scs[h], NEG_INF), diag, states[h])
        o_halves.append(acc_fin[0:DH] / acc_fin[DH:DH + 1])
    o_slc = jnp.concatenate(o_halves, axis=1)

    wstart = pl.multiple_of(i * tq, tq)
    sw = _dot(kwa_ref[pl.ds(wstart, WIN_SPAN), :], qwa_ref[...]) + bw_ref[...]
    pwb = jnp.exp(sw - jnp.max(sw, axis=0, keepdims=True)).astype(BF16)
    wblk = i * (tq // kb_rows)
    vwin = jnp.concatenate([vwt_ref[wblk + kb] for kb in range(WIN_SPAN // kb_rows)], axis=1)
    o_aug = _dot(vwin, pwb)
    o_win = o_aug[0:DH] / o_aug[DH:DH + 1]

    gt = jax.nn.sigmoid(gate_ref[...].astype(F32)).T

    def gate_row(branch):
        return jnp.concatenate([gt[3 * j + branch:3 * j + branch + 1, :] for j in range(NSA_J)], axis=1)

    out_t = gate_row(0) * o_cmp + gate_row(1) * o_slc + gate_row(2) * o_win
    o_ref[...] = jnp.concatenate([out_t[:, j * tq:(j + 1) * tq].T for j in range(NSA_J)], axis=1).astype(o_ref.dtype)


def _nsa(proj, cmp, batch, seq):
    t = proj.shape[0]
    nq = seq // NSA_TQ
    qw = NSA_J * DH
    kv0 = OFF_KV // DH

    def kv_spec(which):
        return pl.BlockSpec((seq, DH), lambda b, g, i: (b, kv0 + 2 * which + g))

    return pl.pallas_call(
        functools.partial(_nsa_body, seq=seq),
        grid=(batch, NSA_GROUPS, nq),
        in_specs=[
            pl.BlockSpec((NSA_TQ, qw), lambda b, g, i: (b * nq + i, g)),
            pl.BlockSpec((NSA_TQ, LANE), lambda b, g, i: (b * nq + i, OFF_NG // LANE + g)),
            pl.BlockSpec((None, None, None, seq // CMP_STRIDE, DH), lambda b, g, i: (b, 0, g, 0, 0)),
            pl.BlockSpec((None, None, None, seq // CMP_STRIDE, DH), lambda b, g, i: (b, 1, g, 0, 0)),
            kv_spec(2), kv_spec(3), kv_spec(4), kv_spec(5),
        ],
        out_specs=pl.BlockSpec((NSA_TQ, qw), lambda b, g, i: (b * nq + i, g)),
        out_shape=jax.ShapeDtypeStruct((t, NSA_Q), BF16),
        scratch_shapes=[
            pltpu.VMEM((seq, AUG), BF16),
            pltpu.VMEM((seq + WINDOW, AUG), BF16),
            pltpu.VMEM((seq // KEY_BLK, VT_ROWS, KEY_BLK), BF16),
            pltpu.VMEM((seq // KEY_BLK + WIN_PAD_BLOCKS, VT_ROWS, KEY_BLK), BF16),
            pltpu.VMEM((DH, seq // CMP_STRIDE), BF16),
            pltpu.VMEM((WIN_SPAN, NSA_COLS), F32),
            pltpu.VMEM((AUG, NSA_COLS), BF16),
            pltpu.VMEM((AUG, NSA_COLS), BF16),
        ],
        compiler_params=_params(("parallel", "arbitrary", "arbitrary")),
        name="nsa_attn",
    )(proj, proj, cmp, cmp, proj, proj, proj, proj)


SSD_W = SSD_HG * SSD_P


def _ssd_group(g, conv, a_cum, a_cum_t, dt, z_ref, dskip_ref, nw_ref, state_ref, o_ref, causal):
    lc = CHUNK
    ch = slice(g * SSD_W, (g + 1) * SSD_W)
    xs = conv[:, ch]
    bm = conv[:, SSD_INNER + g * SSD_N:SSD_INNER + (g + 1) * SSD_N]
    cm = conv[:, SSD_INNER + SSD_BC + g * SSD_N:SSD_INNER + SSD_BC + (g + 1) * SSD_N]
    eh = lax.broadcasted_iota(jnp.int32, (LANE, SSD_W), 0)
    ec = lax.broadcasted_iota(jnp.int32, (LANE, SSD_W), 1) // SSD_P + g * SSD_HG
    spread = jnp.where(eh == ec, 1.0, 0.0).astype(BF16)
    dt_x = _dot_exact_rhs(dt, spread)
    acum_x = _dot_exact_rhs(a_cum, spread)
    last_x = acum_x[lc - 1:lc, :]

    xdt = xs * dt_x
    xdt_bf = xdt.astype(BF16)
    cm_bf = cm.astype(BF16)
    cb = _dot_nt(cm_bf, bm.astype(BF16))

    y_heads = []
    for h in range(SSD_HG):
        col = acum_x[:, h * SSD_P:h * SSD_P + 1]
        rowv = a_cum_t[g * SSD_HG + h:g * SSD_HG + h + 1, :]
        decay = jnp.exp(jnp.where(causal, col - rowv, -jnp.inf))
        y_heads.append(_dot((cb * decay).astype(BF16), xdt_bf[:, h * SSD_P:(h + 1) * SSD_P]))
    y_diag = jnp.concatenate(y_heads, axis=-1)

    state = state_ref[:, ch]
    y_off = _dot(cm_bf, state.astype(BF16)) * jnp.exp(acum_x)
    to_end = jnp.exp(last_x - acum_x)
    chunk_state = _dot(bm.T.astype(BF16), (xdt * to_end).astype(BF16))
    state_ref[:, ch] = state * jnp.exp(last_x) + chunk_state

    y = y_diag + y_off + xs * dskip_ref[:, ch]
    y = y * jax.nn.silu(z_ref[:, ch].astype(F32))
    y = y * lax.rsqrt(jnp.mean(y * y, axis=-1, keepdims=True) + EPS)
    o_ref[:, ch] = (y * nw_ref[:, ch]).astype(o_ref.dtype)


def _ssd_body(xbc_ref, z_ref, dt_ref, cw_ref, cb_ref, dtb_ref, alog_ref, dskip_ref, nw_ref, o_ref,
              state_ref, prev_ref, shift_ref):
    first = pl.program_id(1) == 0
    lc = CHUNK
    taps = SSD_CONV - 1

    @pl.when(first)
    def _():
        state_ref[...] = jnp.zeros_like(state_ref)
        prev_ref[...] = jnp.zeros_like(prev_ref)
        r = lax.broadcasted_iota(jnp.int32, (taps * lc, 2 * lc), 0)
        c = lax.broadcasted_iota(jnp.int32, (taps * lc, 2 * lc), 1)
        k = r // lc
        shift_ref[...] = jnp.where(c == lc + (r - k * lc) - (k + 1), 1.0, 0.0).astype(BF16)

    cur = xbc_ref[...]
    shifted = _dot(shift_ref[...], jnp.concatenate([prev_ref[...], cur], axis=0))
    prev_ref[...] = cur
    conv = cb_ref[...] + cw_ref[taps:taps + 1, :] * cur.astype(F32)
    for back in range(1, SSD_CONV):
        conv = conv + cw_ref[taps - back:taps - back + 1, :] * shifted[(back - 1) * lc:back * lc]
    conv = jax.nn.silu(conv)

    dt_all = jax.nn.softplus(dt_ref[...].astype(F32) + dtb_ref[...])
    a_all = dt_all * -jnp.exp(alog_ref[...])
    li = lax.broadcasted_iota(jnp.int32, (lc, lc), 0)
    si = lax.broadcasted_iota(jnp.int32, (lc, lc), 1)
    causal = li >= si
    a_cum_all = _dot_exact_lhs(jnp.where(causal, 1.0, 0.0).astype(BF16), a_all)
    a_cum_t = a_cum_all.T
    for g in range(SSD_GROUPS):
        _ssd_group(g, conv, a_cum_all, a_cum_t, dt_all, z_ref, dskip_ref, nw_ref, state_ref, o_ref, causal)


def _ssd(proj, lw, layer, batch, seq):
    t = proj.shape[0]
    nc = seq // CHUNK

    def rows(b, c):
        return pl.multiple_of((b * nc + c) * CHUNK, CHUNK)

    def window(width, offset):
        return pl.BlockSpec((pl.Element(CHUNK), pl.Element(width)), lambda b, c: (rows(b, c), offset))

    return pl.pallas_call(
        _ssd_body,
        grid=(batch, nc),
        in_specs=[
            window(SSD_CONV_DIM, OFF_XBC),
            window(SSD_INNER, OFF_Z),
            window(LANE, OFF_DT),
            pl.BlockSpec((None, SSD_CONV, SSD_CONV_DIM), lambda b, c: (layer, 0, 0)),
            pl.BlockSpec((None, 1, SSD_CONV_DIM), lambda b, c: (layer, 0, 0)),
            pl.BlockSpec((None, 1, LANE), lambda b, c: (layer, 0, 0)),
            pl.BlockSpec((None, 1, LANE), lambda b, c: (layer, 0, 0)),
            pl.BlockSpec((None, 1, SSD_INNER), lambda b, c: (layer, 0, 0)),
            pl.BlockSpec((None, 1, SSD_INNER), lambda b, c: (layer, 0, 0)),
        ],
        out_specs=pl.BlockSpec((CHUNK, SSD_INNER), lambda b, c: (b * nc + c, 0)),
        out_shape=jax.ShapeDtypeStruct((t, SSD_INNER), BF16),
        scratch_shapes=[
            pltpu.VMEM((SSD_N, SSD_INNER), F32),
            pltpu.VMEM((CHUNK, SSD_CONV_DIM), BF16),
            pltpu.VMEM(((SSD_CONV - 1) * CHUNK, 2 * CHUNK), BF16),
        ],
        compiler_params=_params(("parallel", "arbitrary")),
        name="ssd",
    )(proj, proj, proj, lw["conv_w"], lw["conv_b"], lw["dt_bias"], lw["a_log"], lw["d_skip"], lw["ssd_norm"])


def _ret_log_gamma(h):
    return float(np.log1p(-np.exp2(-5.0 - h)))


def _ret_body(q_ref, k_ref, v0_ref, v1_ref, g0_ref, g1_ref, nw_ref, o_ref, state_ref):
    lc = CHUNK
    half = RET_HEADS // 2
    v_refs = (v0_ref, v1_ref)
    g_refs = (g0_ref, g1_ref)

    @pl.when(pl.program_id(1) == 0)
    def _():
        state_ref[...] = jnp.zeros_like(state_ref)

    li = lax.broadcasted_iota(jnp.int32, (lc, lc), 0)
    si = lax.broadcasted_iota(jnp.int32, (lc, lc), 1)
    diff = (li - si).astype(F32)
    idx = lax.broadcasted_iota(jnp.int32, (lc, 1), 0).astype(F32)

    for h in range(RET_HEADS):
        log_g = _ret_log_gamma(h)
        dmat = jnp.where(diff >= 0, jnp.exp(diff * log_g), 0.0)
        k_dec = jnp.exp((lc - 1.0 - idx) * log_g)
        q_dec = jnp.exp((idx + 1.0) * log_g)
        q = q_ref[:, h * RET_DK:(h + 1) * RET_DK].astype(F32) * (RET_DK ** -0.5)
        k = k_ref[:, h * RET_DK:(h + 1) * RET_DK].astype(F32)
        vsl = slice((h % half) * RET_DV, (h % half + 1) * RET_DV)
        v = v_refs[h // half][:, vsl]
        scores = _dot_nt(q.astype(BF16), k.astype(BF16)) * dmat
        state = state_ref[h]
        y = _dot(scores.astype(BF16), v) + _dot((q * q_dec).astype(BF16), state.astype(BF16))
        chunk_kv = _dot((k * k_dec).T.astype(BF16), v)
        state_ref[h] = state * float(np.exp(lc * log_g)) + chunk_kv

        mu = jnp.mean(y, axis=-1, keepdims=True)
        yc = y - mu
        var = jnp.mean(yc * yc, axis=-1, keepdims=True)
        y = yc * lax.rsqrt(var + EPS) * nw_ref[h]
        gate = jax.nn.silu(g_refs[h // half][:, vsl].astype(F32))
        o_ref[:, h * RET_DV:(h + 1) * RET_DV] = (gate * y).astype(o_ref.dtype)


def _retention(proj, nw, layer, batch, seq):
    t = proj.shape[0]
    nc = seq // CHUNK
    hw = RET_V // 2
    q0 = OFF_RQK // RET_QK
    v0 = OFF_RV // hw
    g0 = OFF_RG // hw
    return pl.pallas_call(
        _ret_body,
        grid=(batch, nc),
        in_specs=[
            pl.BlockSpec((CHUNK, RET_QK), lambda b, c: (b * nc + c, q0)),
            pl.BlockSpec((CHUNK, RET_QK), lambda b, c: (b * nc + c, q0 + 1)),
            pl.BlockSpec((CHUNK, hw), lambda b, c: (b * nc + c, v0)),
            pl.BlockSpec((CHUNK, hw), lambda b, c: (b * nc + c, v0 + 1)),
            pl.BlockSpec((CHUNK, hw), lambda b, c: (b * nc + c, g0)),
            pl.BlockSpec((CHUNK, hw), lambda b, c: (b * nc + c, g0 + 1)),
            pl.BlockSpec((None, RET_HEADS, 1, RET_DV), lambda b, c: (layer, 0, 0, 0)),
        ],
        out_specs=pl.BlockSpec((CHUNK, RET_V), lambda b, c: (b * nc + c, 0)),
        out_shape=jax.ShapeDtypeStruct((t, RET_V), BF16),
        scratch_shapes=[pltpu.VMEM((RET_HEADS, RET_DK, RET_DV), F32)],
        compiler_params=_params(("parallel", "arbitrary")),
        name="retention",
    )(proj, proj, proj, proj, proj, proj, nw)


def _per_group_rows(seg, groups):
    depth, rows, d = seg.shape
    per = rows // groups
    seg = seg.reshape(depth, groups, per, d)
    seg = jnp.pad(seg, [(0, 0), (0, 0), (0, LANE - per), (0, 0)])
    return seg.reshape(depth, groups * LANE, d)


def _small_w_t(w_t):
    o_ng = IN_SPLITS[0] + IN_SPLITS[1]
    o_dt = o_ng + IN_SPLITS[2] + IN_SPLITS[3] + IN_SPLITS[4]
    ng = _per_group_rows(w_t[:, o_ng:o_ng + IN_SPLITS[2], :], NSA_GROUPS)
    dt = w_t[:, o_dt:o_dt + IN_SPLITS[5], :]
    pad = jnp.zeros((w_t.shape[0], N_SMALL - ng.shape[1] - dt.shape[1], w_t.shape[2]), w_t.dtype)
    return jnp.concatenate([ng, dt, pad], axis=1)


def _pad_lanes(v):
    depth, heads = v.shape
    return jnp.pad(v, [(0, 0), (0, LANE - heads)]).reshape(depth, 1, LANE)


def _layer(x, lw, layer, batch, seq):
    u = _rms(x, lw["norm_mix"], layer, BF16)
    proj, xc = _inproj(u, lw["w_t"], lw["w_small_t"], layer)
    cmp = _compress(xc, lw["cmp_pe"][layer], lw["cmp_w1"][layer], lw["cmp_w2"][layer], batch)

    y_nsa = _nsa(proj, cmp, batch, seq)
    y_ssd = _ssd(proj, lw, layer, batch, seq)
    y_ret = _retention(proj, lw["ret_norm"], layer, batch, seq)

    merged = _merge(y_nsa, y_ssd, y_ret, proj, lw["p_nsa"], lw["p_ssd"], lw["p_ret"], layer)
    x, f = _mm_res_norm(merged, lw["w_out"], x, lw["norm_ffn"], layer, 512)
    hidden = _ffn_up(f, lw["w_gate"], lw["w_up"], layer)
    return _mm_res(hidden, lw["w_down"], x, layer, 256)


def kernel(x, norm_mix, w_in, cmp_k_pe, cmp_k_w1, cmp_k_w2, cmp_v_pe, cmp_v_w1, cmp_v_w2, conv_w, conv_b, dt_bias, a_log, d_skip, ssd_norm, ret_norm, p_nsa, p_ssd, p_ret, w_out, norm_ffn, w_gate, w_up, w_down, norm_final):
    batch, seq, d = x.shape
    depth = w_in.shape[0]
    w_t = jnp.swapaxes(w_in, 1, 2)
    pe = jnp.stack([cmp_k_pe, cmp_v_pe], axis=1).reshape(depth, 2, 1, CMP_LEN * DH)
    lw = {
        "norm_mix": norm_mix[:, None, :],
        "w_t": w_t,
        "w_small_t": _small_w_t(w_t),
        "cmp_pe": jnp.pad(pe, [(0, 0), (0, 0), (0, 7), (0, 0)]).astype(BF16),
        "cmp_w1": jnp.stack([cmp_k_w1, cmp_v_w1], axis=1).astype(BF16),
        "cmp_w2": jnp.stack([cmp_k_w2, cmp_v_w2], axis=1).astype(BF16),
        "conv_w": conv_w,
        "conv_b": conv_b[:, None, :],
        "dt_bias": _pad_lanes(dt_bias),
        "a_log": _pad_lanes(a_log),
        "d_skip": jnp.repeat(d_skip, SSD_P, axis=-1)[:, None, :],
        "ssd_norm": ssd_norm[:, None, :],
        "ret_norm": ret_norm[:, :, None, :],
        "p_nsa": p_nsa, "p_ssd": p_ssd, "p_ret": p_ret, "w_out": w_out,
        "norm_ffn": norm_ffn[:, None, :],
        "w_gate": w_gate, "w_up": w_up, "w_down": w_down,
    }
    h = x.reshape(batch * seq, d)
    for layer in range(depth):
        h = _layer(h, lw, layer, batch, seq)
    return _rms(h, norm_final[None, None, :], 0, F32).reshape(batch, seq, d)
```

```python
import functools

import jax
import jax.numpy as jnp
import numpy as np
from jax import lax
from jax.experimental import pallas as pl
from jax.experimental.pallas import tpu as pltpu

F32 = jnp.float32
BF16 = jnp.bfloat16

D_MODEL = 2048
NSA_HEADS = 8
NSA_GROUPS = 2
NSA_J = NSA_HEADS // NSA_GROUPS
DH = 128
CMP_LEN = 32
CMP_STRIDE = 16
SLC_BLOCK = 64
SLC_TOPK = 8
WINDOW = 512
SSD_INNER = D_MODEL
SSD_P = 64
SSD_HEADS = SSD_INNER // SSD_P
SSD_GROUPS = 4
SSD_HG = SSD_HEADS // SSD_GROUPS
SSD_N = 128
SSD_CONV = 4
CHUNK = 128
RET_HEADS = 4
RET_DK = 128
RET_DV = 256
D_FF = -(-8 * D_MODEL // (3 * 256)) * 256
NSA_Q = NSA_HEADS * DH
NSA_KV = NSA_GROUPS * DH
SSD_BC = SSD_GROUPS * SSD_N
SSD_CONV_DIM = SSD_INNER + 2 * SSD_BC
RET_QK = RET_HEADS * RET_DK
RET_V = RET_HEADS * RET_DV
IN_SPLITS = (NSA_Q, 6 * NSA_KV, 3 * NSA_HEADS, SSD_INNER, SSD_CONV_DIM, SSD_HEADS,
             2 * RET_QK, RET_V, RET_V, 3 * D_MODEL)
D_IN = sum(IN_SPLITS)
NEG_INF = -1e30
FORCE_SCORE = 1e9
BELOW_ALL = -3e38
EPS = 1e-6

LANE = 128

OFF_Q = 0
OFF_KV = OFF_Q + NSA_Q
OFF_Z = OFF_KV + 6 * NSA_KV
OFF_XBC = OFF_Z + SSD_INNER
OFF_RQK = OFF_XBC + SSD_CONV_DIM
OFF_RV = OFF_RQK + 2 * RET_QK
OFF_RG = OFF_RV + RET_V
OFF_MG = OFF_RG + RET_V
OFF_NG = OFF_MG + 3 * D_MODEL
OFF_DT = OFF_NG + NSA_GROUPS * LANE
N_USED = OFF_DT + LANE
PROJ_TN = 512
N_PACK = -(-N_USED // PROJ_TN) * PROJ_TN
N_MAIN_BLOCKS = OFF_NG // PROJ_TN
N_SMALL = N_PACK - OFF_NG

SHIFT_B = IN_SPLITS[2]
SHIFT_C = IN_SPLITS[2] + IN_SPLITS[5]
BLK_B = OFF_Z // PROJ_TN
BLK_C = OFF_RQK // PROJ_TN

ROW_TILE = 2048
VMEM_LIMIT = 56 * 1024 * 1024


def _dot(a, b):
    return jnp.dot(a, b, preferred_element_type=F32)


def _dot_nt(a, b):
    return lax.dot_general(a, b, (((1,), (1,)), ((), ())), preferred_element_type=F32)


def _split3(x):
    hi = x.astype(BF16)
    r1 = x - hi.astype(F32)
    mid = r1.astype(BF16)
    lo = (r1 - mid.astype(F32)).astype(BF16)
    return hi, mid, lo


def _dot_exact_rhs(x, m_bf16):
    hi, mid, lo = _split3(x)
    return _dot(hi, m_bf16) + _dot(mid, m_bf16) + _dot(lo, m_bf16)


def _dot_exact_lhs(m_bf16, x):
    hi, mid, lo = _split3(x)
    return _dot(m_bf16, hi) + _dot(m_bf16, mid) + _dot(m_bf16, lo)


def _params(sem):
    return pltpu.CompilerParams(dimension_semantics=sem, vmem_limit_bytes=VMEM_LIMIT)


def _rms_body(x_ref, nw_ref, o_ref):
    slab = 128

    def body(r, carry):
        sl = pl.ds(pl.multiple_of(r * slab, slab), slab)
        x = x_ref[sl, :]
        ms = jnp.mean(x * x, axis=-1, keepdims=True)
        o_ref[sl, :] = (x * lax.rsqrt(ms + EPS) * nw_ref[...]).astype(o_ref.dtype)
        return carry

    lax.fori_loop(0, x_ref.shape[0] // slab, body, 0)


def _rms(x, nw, layer, out_dtype):
    t, d = x.shape
    tm = 1024
    return pl.pallas_call(
        _rms_body,
        grid=(t // tm,),
        in_specs=[pl.BlockSpec((tm, d), lambda i: (i, 0)),
                  pl.BlockSpec((None, 1, d), lambda i: (layer, 0, 0))],
        out_specs=pl.BlockSpec((tm, d), lambda i: (i, 0)),
        out_shape=jax.ShapeDtypeStruct((t, d), out_dtype),
        compiler_params=_params(("parallel",)),
        name="rmsnorm",
    )(x, nw)


CMP_BLK = OFF_KV // PROJ_TN
CMP_STREAMS = 2 * NSA_GROUPS
CMP_CHUNK_W = CMP_STRIDE * DH


def _inproj_body(u_ref, w_ref, ws_ref, o_ref, xc_ref, stage_ref):
    j = pl.program_id(1)

    @pl.when(j < N_MAIN_BLOCKS)
    def _():
        acc = _dot_nt(u_ref[...], w_ref[0].astype(BF16))
        o_ref[...] = acc.astype(o_ref.dtype)

        @pl.when(j == CMP_BLK)
        def _():
            nchunk = stage_ref.shape[1] // CMP_STRIDE
            for s in range(CMP_STREAMS):
                stage_ref[s] = acc[:, s * DH:(s + 1) * DH]
            for s in range(CMP_STREAMS):
                for l in range(CMP_STRIDE):
                    lo = s * CMP_CHUNK_W + l * DH
                    rows = stage_ref[s, pl.ds(l, nchunk, stride=CMP_STRIDE), :]
                    xc_ref[:, lo:lo + DH] = rows.astype(xc_ref.dtype)

    @pl.when(j >= N_MAIN_BLOCKS)
    def _():
        o_ref[...] = _dot_nt(u_ref[...], ws_ref[...].astype(BF16)).astype(o_ref.dtype)


def _inproj(u, w_t, w_small_t, layer):
    t, d = u.shape
    tm, tn = ROW_TILE, PROJ_TN
    assert OFF_KV == CMP_BLK * tn and CMP_STREAMS * DH == tn
    last_main = N_MAIN_BLOCKS - 1

    def w_index(i, j):
        jc = jnp.minimum(j, last_main)
        shift = jnp.where(jc >= BLK_C, SHIFT_C, jnp.where(jc >= BLK_B, SHIFT_B, 0))
        return (layer, pl.multiple_of(jc * tn + shift, 8), 0)

    return pl.pallas_call(
        _inproj_body,
        grid=(t // tm, N_PACK // tn),
        in_specs=[
            pl.BlockSpec((tm, d), lambda i, j: (i, 0)),
            pl.BlockSpec((pl.Element(1), pl.Element(tn), pl.Element(d)), w_index),
            pl.BlockSpec((None, tn, d), lambda i, j: (layer, jnp.maximum(j - N_MAIN_BLOCKS, 0), 0)),
        ],
        out_specs=[
            pl.BlockSpec((tm, tn), lambda i, j: (i, j)),
            pl.BlockSpec((tm // CMP_STRIDE, CMP_STREAMS * CMP_CHUNK_W), lambda i, j: (i, 0)),
        ],
        out_shape=[
            jax.ShapeDtypeStruct((t, N_PACK), BF16),
            jax.ShapeDtypeStruct((t // CMP_STRIDE, CMP_STREAMS * CMP_CHUNK_W), BF16),
        ],
        scratch_shapes=[pltpu.VMEM((CMP_STREAMS, tm, DH), F32)],
        compiler_params=_params(("parallel", "arbitrary")),
        name="inproj",
    )(u, w_t, w_small_t)


def _merge_body(yn_ref, ys_ref, yr_ref, gn_ref, gs_ref, gr_ref, pn_ref, ps_ref, pr_ref, o_ref):
    def gate(g_ref):
        return jax.nn.sigmoid(g_ref[...].astype(F32))

    acc = gate(gn_ref) * _dot(yn_ref[...], pn_ref[...].astype(BF16))
    acc = acc + gate(gs_ref) * _dot(ys_ref[...], ps_ref[...].astype(BF16))
    acc = acc + gate(gr_ref) * _dot(yr_ref[...], pr_ref[...].astype(BF16))
    o_ref[...] = acc.astype(o_ref.dtype)


def _merge(y_nsa, y_ssd, y_ret, proj, p_nsa, p_ssd, p_ret, layer):
    t = y_nsa.shape[0]
    tm, tn = 1024, 512
    g0 = OFF_MG // tn
    gstep = D_MODEL // tn
    return pl.pallas_call(
        _merge_body,
        grid=(t // tm, D_MODEL // tn),
        in_specs=[
            pl.BlockSpec((tm, NSA_Q), lambda i, j: (i, 0)),
            pl.BlockSpec((tm, SSD_INNER), lambda i, j: (i, 0)),
            pl.BlockSpec((tm, RET_V), lambda i, j: (i, 0)),
            pl.BlockSpec((tm, tn), lambda i, j: (i, g0 + j)),
            pl.BlockSpec((tm, tn), lambda i, j: (i, g0 + gstep + j)),
            pl.BlockSpec((tm, tn), lambda i, j: (i, g0 + 2 * gstep + j)),
            pl.BlockSpec((None, NSA_Q, tn), lambda i, j: (layer, 0, j)),
            pl.BlockSpec((None, SSD_INNER, tn), lambda i, j: (layer, 0, j)),
            pl.BlockSpec((None, RET_V, tn), lambda i, j: (layer, 0, j)),
        ],
        out_specs=pl.BlockSpec((tm, tn), lambda i, j: (i, j)),
        out_shape=jax.ShapeDtypeStruct((t, D_MODEL), BF16),
        compiler_params=_params(("parallel", "arbitrary")),
        name="merge",
    )(y_nsa, y_ssd, y_ret, proj, proj, proj, p_nsa, p_ssd, p_ret)


def _mm_res_body(a_ref, w_ref, r_ref, o_ref):
    o_ref[...] = r_ref[...] + _dot(a_ref[...], w_ref[...].astype(BF16))


def _mm_res(a, w, res, layer, tn):
    t, k = a.shape
    n = w.shape[2]
    tm = ROW_TILE if k <= D_MODEL else ROW_TILE // 2
    return pl.pallas_call(
        _mm_res_body,
        grid=(t // tm, n // tn),
        in_specs=[
            pl.BlockSpec((tm, k), lambda i, j: (i, 0)),
            pl.BlockSpec((None, k, tn), lambda i, j: (layer, 0, j)),
            pl.BlockSpec((tm, tn), lambda i, j: (i, j)),
        ],
        out_specs=pl.BlockSpec((tm, tn), lambda i, j: (i, j)),
        out_shape=jax.ShapeDtypeStruct((t, n), F32),
        compiler_params=_params(("parallel", "arbitrary")),
        name="mm_res",
    )(a, w, res)


def _ffn_up_body(f_ref, wg_ref, wu_ref, o_ref):
    f = f_ref[...]
    half = o_ref.shape[1] // 2
    for h in range(2):
        cols = slice(h * half, (h + 1) * half)
        gate = _dot(f, wg_ref[:, cols].astype(BF16))
        up = _dot(f, wu_ref[:, cols].astype(BF16))
        o_ref[:, cols] = (jax.nn.silu(gate) * up).astype(o_ref.dtype)


def _ffn_up(f, wg, wu, layer):
    t, d = f.shape
    n = wg.shape[2]
    tm, tn = ROW_TILE, 512
    return pl.pallas_call(
        _ffn_up_body,
        grid=(t // tm, n // tn),
        in_specs=[
            pl.BlockSpec((tm, d), lambda i, j: (i, 0)),
            pl.BlockSpec((None, d, tn), lambda i, j: (layer, 0, j)),
            pl.BlockSpec((None, d, tn), lambda i, j: (layer, 0, j)),
        ],
        out_specs=pl.BlockSpec((tm, tn), lambda i, j: (i, j)),
        out_shape=jax.ShapeDtypeStruct((t, n), BF16),
        compiler_params=_params(("parallel", "arbitrary")),
        name="ffn_up",
    )(f, wg, wu)


def _compress_body(x_ref, pe_ref, w1_ref, w2_ref, o_ref):
    half = (CMP_LEN // 2) * DH
    x = x_ref[...]
    a = _dot(x, w1_ref[:half, :])
    b = _dot(x, w1_ref[half:, :])
    bias = _dot(pe_ref[...], w1_ref[...])[0:1, :]
    nblk = x.shape[0]
    h = a + pltpu.roll(b, nblk - 1, 0) + bias
    h = jax.nn.gelu(h)
    o_ref[...] = _dot(h.astype(BF16), w2_ref[...]).astype(o_ref.dtype)


def _compress(xc, pe, w1, w2, batch):
    b, g = batch, NSA_GROUPS
    nchunk = xc.shape[0] // batch
    width = CMP_CHUNK_W
    return pl.pallas_call(
        _compress_body,
        grid=(b, 2, g),
        in_specs=[
            pl.BlockSpec((nchunk, width), lambda bi, m, gi: (bi, m * NSA_GROUPS + gi)),
            pl.BlockSpec((None, 8, CMP_LEN * DH), lambda bi, m, gi: (m, 0, 0)),
            pl.BlockSpec((None, CMP_LEN * DH, DH), lambda bi, m, gi: (m, 0, 0)),
            pl.BlockSpec((None, DH, DH), lambda bi, m, gi: (m, 0, 0)),
        ],
        out_specs=pl.BlockSpec((None, None, None, nchunk, DH), lambda bi, m, gi: (bi, m, gi, 0, 0)),
        out_shape=jax.ShapeDtypeStruct((b, 2, g, nchunk, DH), BF16),
        compiler_params=_params(("parallel", "arbitrary", "arbitrary")),
        name="nsa_compress",
    )(xc, pe, w1, w2)


NSA_TQ = 256
KEY_BLK = 256
VT_ROWS = DH + 16
NSA_COLS = NSA_J * NSA_TQ
SLC_KCHUNK = 256
WIN_SPAN = WINDOW + NSA_TQ
WIN_PAD_BLOCKS = WINDOW // KEY_BLK
AUG = 2 * DH
COL_BLK = SLC_BLOCK // 2
COL_OFF = COL_BLK + 1


def _nsa_body(q_ref, gate_ref, kc_ref, vc_ref, ks_ref, vs_ref, kw_ref, vw_ref, o_ref,
              ksa_ref, kwa_ref, vst_ref, vwt_ref, vct_ref, bw_ref, qsa_ref, qwa_ref, *, seq):
    g = pl.program_id(1)
    i = pl.program_id(2)
    tq = NSA_TQ
    cols = NSA_COLS
    n_slc = seq // SLC_BLOCK
    kb_rows = KEY_BLK
    nkb = seq // kb_rows

    cc = lax.broadcasted_iota(jnp.int32, (1, cols), 1)
    head = cc // tq
    tl = cc - head * tq
    slope_g = jnp.where(g == 0, 0.5, 0.5 ** (NSA_J + 1)).astype(F32)
    slope = slope_g * jnp.where(head == 0, 1.0, jnp.where(head == 1, 0.5, jnp.where(head == 2, 0.25, 0.125)))
    tl_f = tl.astype(F32)

    @pl.when(i == 0)
    def _init():
        def blk_body(kb, carry):
            r0 = pl.multiple_of(kb * kb_rows, kb_rows)
            rows = pl.ds(r0, kb_rows)
            key = r0 + lax.broadcasted_iota(jnp.int32, (kb_rows, DH), 0)
            col = lax.broadcasted_iota(jnp.int32, (kb_rows, DH), 1)
            blk = key // SLC_BLOCK
            off = key - blk * SLC_BLOCK
            extra = jnp.where(col == blk, 1.0, 0.0)
            extra = jnp.where(col == COL_BLK, blk.astype(F32), extra)
            extra = jnp.where(col == COL_OFF, off.astype(F32), extra)
            ksa_ref[rows, 0:DH] = ks_ref[rows, :]
            ksa_ref[rows, DH:AUG] = extra.astype(BF16)
            prow = pl.ds(pl.multiple_of(r0 + WINDOW, kb_rows), kb_rows)
            kwa_ref[prow, 0:DH] = kw_ref[rows, :]
            kwa_ref[prow, DH:AUG] = jnp.zeros((kb_rows, DH), BF16)
            orow = lax.broadcasted_iota(jnp.int32, (VT_ROWS - DH, kb_rows), 0)
            ones_row = jnp.where(orow == 0, 1.0, 0.0).astype(BF16)
            vst_ref[kb, 0:DH, :] = vs_ref[rows, :].astype(F32).T.astype(BF16)
            vst_ref[kb, DH:VT_ROWS, :] = ones_row
            vwt_ref[kb + WIN_PAD_BLOCKS, 0:DH, :] = vw_ref[rows, :].astype(F32).T.astype(BF16)
            vwt_ref[kb + WIN_PAD_BLOCKS, DH:VT_ROWS, :] = ones_row
            return carry

        lax.fori_loop(0, nkb, blk_body, 0)
        pcol = lax.broadcasted_iota(jnp.int32, (WINDOW, DH), 1)
        kwa_ref[0:WINDOW, 0:DH] = jnp.zeros((WINDOW, DH), BF16)
        kwa_ref[0:WINDOW, DH:AUG] = jnp.where(pcol == 0, 1.0, 0.0).astype(BF16)
        vwt_ref[0:WIN_PAD_BLOCKS] = jnp.zeros((WIN_PAD_BLOCKS, VT_ROWS, kb_rows), BF16)
        vct_ref[...] = vc_ref[...].astype(F32).T.astype(BF16)
        wk = lax.broadcasted_iota(jnp.int32, (WIN_SPAN, cols), 0)
        d = (WINDOW + tl - wk).astype(F32)
        bw_ref[...] = jnp.where((d >= 0) & (d < WINDOW), -slope * d, NEG_INF)

    q = q_ref[...].astype(F32) * (DH ** -0.5)
    qt = jnp.concatenate([q[:, j * DH:(j + 1) * DH].T for j in range(NSA_J)], axis=1).astype(BF16)
    qsa_ref[0:DH, :] = qt
    qwa_ref[0:DH, :] = qt
    er = lax.broadcasted_iota(jnp.int32, (DH, cols), 0)
    qwa_ref[DH:AUG, :] = jnp.where(er == 0, NEG_INF, 0.0).astype(BF16)

    ncmp = kc_ref.shape[0]
    n_idx = lax.broadcasted_iota(jnp.int32, (ncmp, 1), 0)
    rel_end = (n_idx * CMP_STRIDE + (CMP_LEN - 1)).astype(F32) - tl_f
    base = (i * tq).astype(F32)
    s = _dot(kc_ref[...], qt) + slope * rel_end
    s = jnp.where(rel_end <= base, s, NEG_INF)
    p = jnp.exp(s - jnp.max(s, axis=0, keepdims=True))
    p = p / jnp.sum(p, axis=0, keepdims=True)
    p = p * ((base + tl_f) >= (CMP_LEN - 1)).astype(F32)
    o_cmp = _dot(vct_ref[...], p.astype(BF16))

    psum = p[:, 0:tq]
    for j in range(1, NSA_J):
        psum = psum + p[:, j * tq:(j + 1) * tq]
    kk = lax.broadcasted_iota(jnp.int32, (LANE, ncmp), 0)
    nn = lax.broadcasted_iota(jnp.int32, (LANE, ncmp), 1)
    overlap = ((nn * CMP_STRIDE < (kk + 1) * SLC_BLOCK)
               & (nn * CMP_STRIDE + (CMP_LEN - 1) >= kk * SLC_BLOCK)
               & (kk < n_slc))
    imp = _dot_exact_lhs(jnp.where(overlap, 1.0, 0.0).astype(BF16), psum)[0:n_slc, :]
    blk = lax.broadcasted_iota(jnp.int32, (n_slc, tq), 0)
    blk_f = blk.astype(F32)
    cur = (i * tq + lax.broadcasted_iota(jnp.int32, (1, tq), 1)) // SLC_BLOCK
    val = jnp.where(blk > cur, NEG_INF, imp)
    val = jnp.where((blk == cur) | (blk == 0), FORCE_SCORE, val)
    sel = jnp.zeros((n_slc, tq), F32)
    for _ in range(min(SLC_TOPK, n_slc)):
        mx = jnp.max(val, axis=0, keepdims=True)
        idx = jnp.min(jnp.where(val == mx, blk_f, float(n_slc)), axis=0, keepdims=True)
        hit = blk_f == idx
        sel = jnp.where(hit, 1.0, sel)
        val = jnp.where(hit, BELOW_ALL, val)
    pen = (sel - 1.0) * (-NEG_INF)
    pen = jnp.concatenate([pen] * NSA_J, axis=1)
    er2 = lax.broadcasted_iota(jnp.int32, (DH - n_slc, cols), 0) + n_slc
    tail = jnp.where(er2 == COL_BLK, slope * SLC_BLOCK, jnp.where(er2 == COL_OFF, slope, 0.0))
    qsa_ref[DH:AUG, :] = jnp.concatenate([pen, tail], axis=0).astype(BF16)

    hw = 4 * LANE
    nhalf = cols // hw

    def slc_update(sc, c, state):
        m_run, acc = state
        m_new = jnp.maximum(m_run, jnp.max(sc, axis=0, keepdims=True))
        alpha = jnp.exp(m_run - m_new)
        prb = jnp.exp(sc - m_new).astype(BF16)
        return m_new, alpha * acc + _dot(vst_ref[c], prb)

    def slc_scores(c, h):
        start = pl.multiple_of(c * SLC_KCHUNK, SLC_KCHUNK)
        return _dot(ksa_ref[pl.ds(start, SLC_KCHUNK), :], qsa_ref[:, h * hw:(h + 1) * hw])

    def slc_step(c, carry):
        scs, states = carry
        nxt = tuple(slc_scores(c + 1, h) for h in range(nhalf))
        return nxt, tuple(slc_update(scs[h], c, states[h]) for h in range(nhalf))

    diag = (i * tq) // SLC_KCHUNK
    state0 = (jnp.full((1, hw), NEG_INF, F32), jnp.zeros((VT_ROWS, hw), F32))
    init = (tuple(slc_scores(0, h) for h in range(nhalf)), (state0,) * nhalf)
    scs, states = lax.fori_loop(0, diag, slc_step, init)
    krow = diag * SLC_KCHUNK + lax.broadcasted_iota(jnp.int32, (SLC_KCHUNK, 1), 0)
    o_halves = []
    for h in range(nhalf):
        causal = krow <= i * tq + tl[:, h * hw:(h + 1) * hw]
        _, acc_fin = slc_update(jnp.where(causal, scs[h], NEG_INF), diag, states[h])
        o_halves.append(acc_fin[0:DH] / acc_fin[DH:DH + 1])
    o_slc = jnp.concatenate(o_halves, axis=1)

    wstart = pl.multiple_of(i * tq, tq)
    sw = _dot(kwa_ref[pl.ds(wstart, WIN_SPAN), :], qwa_ref[...]) + bw_ref[...]
    pwb = jnp.exp(sw - jnp.max(sw, axis=0, keepdims=True)).astype(BF16)
    wblk = i * (tq // kb_rows)
    vwin = jnp.concatenate([vwt_ref[wblk + kb] for kb in range(WIN_SPAN // kb_rows)], axis=1)
    o_aug = _dot(vwin, pwb)
    o_win = o_aug[0:DH] / o_aug[DH:DH + 1]

    gt = jax.nn.sigmoid(gate_ref[...].astype(F32)).T

    def gate_row(branch):
        return jnp.concatenate([gt[3 * j + branch:3 * j + branch + 1, :] for j in range(NSA_J)], axis=1)

    out_t = gate_row(0) * o_cmp + gate_row(1) * o_slc + gate_row(2) * o_win
    o_ref[...] = jnp.concatenate([out_t[:, j * tq:(j + 1) * tq].T for j in range(NSA_J)], axis=1).astype(o_ref.dtype)


def _nsa(proj, cmp, batch, seq):
    t = proj.shape[0]
    nq = seq // NSA_TQ
    qw = NSA_J * DH
    kv0 = OFF_KV // DH

    def kv_spec(which):
        return pl.BlockSpec((seq, DH), lambda b, g, i: (b, kv0 + 2 * which + g))

    return pl.pallas_call(
        functools.partial(_nsa_body, seq=seq),
        grid=(batch, NSA_GROUPS, nq),
        in_specs=[
            pl.BlockSpec((NSA_TQ, qw), lambda b, g, i: (b * nq + i, g)),
            pl.BlockSpec((NSA_TQ, LANE), lambda b, g, i: (b * nq + i, OFF_NG // LANE + g)),
            pl.BlockSpec((None, None, None, seq // CMP_STRIDE, DH), lambda b, g, i: (b, 0, g, 0, 0)),
            pl.BlockSpec((None, None, None, seq // CMP_STRIDE, DH), lambda b, g, i: (b, 1, g, 0, 0)),
            kv_spec(2), kv_spec(3), kv_spec(4), kv_spec(5),
        ],
        out_specs=pl.BlockSpec((NSA_TQ, qw), lambda b, g, i: (b * nq + i, g)),
        out_shape=jax.ShapeDtypeStruct((t, NSA_Q), BF16),
        scratch_shapes=[
            pltpu.VMEM((seq, AUG), BF16),
            pltpu.VMEM((seq + WINDOW, AUG), BF16),
            pltpu.VMEM((seq // KEY_BLK, VT_ROWS, KEY_BLK), BF16),
            pltpu.VMEM((seq // KEY_BLK + WIN_PAD_BLOCKS, VT_ROWS, KEY_BLK), BF16),
            pltpu.VMEM((DH, seq // CMP_STRIDE), BF16),
            pltpu.VMEM((WIN_SPAN, NSA_COLS), F32),
            pltpu.VMEM((AUG, NSA_COLS), BF16),
            pltpu.VMEM((AUG, NSA_COLS), BF16),
        ],
        compiler_params=_params(("parallel", "arbitrary", "arbitrary")),
        name="nsa_attn",
    )(proj, proj, cmp, cmp, proj, proj, proj, proj)


SSD_W = SSD_HG * SSD_P


def _ssd_group(g, conv, a_cum, a_cum_t, dt, z_ref, dskip_ref, nw_ref, state_ref, o_ref, causal):
    lc = CHUNK
    ch = slice(g * SSD_W, (g + 1) * SSD_W)
    xs = conv[:, ch]
    bm = conv[:, SSD_INNER + g * SSD_N:SSD_INNER + (g + 1) * SSD_N]
    cm = conv[:, SSD_INNER + SSD_BC + g * SSD_N:SSD_INNER + SSD_BC + (g + 1) * SSD_N]
    eh = lax.broadcasted_iota(jnp.int32, (LANE, SSD_W), 0)
    ec = lax.broadcasted_iota(jnp.int32, (LANE, SSD_W), 1) // SSD_P + g * SSD_HG
    spread = jnp.where(eh == ec, 1.0, 0.0).astype(BF16)
    dt_x = _dot_exact_rhs(dt, spread)
    acum_x = _dot_exact_rhs(a_cum, spread)
    last_x = acum_x[lc - 1:lc, :]

    xdt = xs * dt_x
    xdt_bf = xdt.astype(BF16)
    cm_bf = cm.astype(BF16)
    cb = _dot_nt(cm_bf, bm.astype(BF16))

    y_heads = []
    for h in range(SSD_HG):
        col = acum_x[:, h * SSD_P:h * SSD_P + 1]
        rowv = a_cum_t[g * SSD_HG + h:g * SSD_HG + h + 1, :]
        decay = jnp.exp(jnp.where(causal, col - rowv, -jnp.inf))
        y_heads.append(_dot((cb * decay).astype(BF16), xdt_bf[:, h * SSD_P:(h + 1) * SSD_P]))
    y_diag = jnp.concatenate(y_heads, axis=-1)

    state = state_ref[:, ch]
    y_off = _dot(cm_bf, state.astype(BF16)) * jnp.exp(acum_x)
    to_end = jnp.exp(last_x - acum_x)
    chunk_state = _dot(bm.T.astype(BF16), (xdt * to_end).astype(BF16))
    state_ref[:, ch] = state * jnp.exp(last_x) + chunk_state

    y = y_diag + y_off + xs * dskip_ref[:, ch]
    y = y * jax.nn.silu(z_ref[:, ch].astype(F32))
    y = y * lax.rsqrt(jnp.mean(y * y, axis=-1, keepdims=True) + EPS)
    o_ref[:, ch] = (y * nw_ref[:, ch]).astype(o_ref.dtype)


def _ssd_body(xbc_ref, z_ref, dt_ref, cw_ref, cb_ref, dtb_ref, alog_ref, dskip_ref, nw_ref, o_ref,
              state_ref, prev_ref, shift_ref):
    first = pl.program_id(1) == 0
    lc = CHUNK
    taps = SSD_CONV - 1

    @pl.when(first)
    def _():
        state_ref[...] = jnp.zeros_like(state_ref)
        prev_ref[...] = jnp.zeros_like(prev_ref)
        r = lax.broadcasted_iota(jnp.int32, (taps * lc, 2 * lc), 0)
        c = lax.broadcasted_iota(jnp.int32, (taps * lc, 2 * lc), 1)
        k = r // lc
        shift_ref[...] = jnp.where(c == lc + (r - k * lc) - (k + 1), 1.0, 0.0).astype(BF16)

    cur = xbc_ref[...]
    shifted = _dot(shift_ref[...], jnp.concatenate([prev_ref[...], cur], axis=0))
    prev_ref[...] = cur
    conv = cb_ref[...] + cw_ref[taps:taps + 1, :] * cur.astype(F32)
    for back in range(1, SSD_CONV):
        conv = conv + cw_ref[taps - back:taps - back + 1, :] * shifted[(back - 1) * lc:back * lc]
    conv = jax.nn.silu(conv)

    dt_all = jax.nn.softplus(dt_ref[...].astype(F32) + dtb_ref[...])
    a_all = dt_all * -jnp.exp(alog_ref[...])
    li = lax.broadcasted_iota(jnp.int32, (lc, lc), 0)
    si = lax.broadcasted_iota(jnp.int32, (lc, lc), 1)
    causal = li >= si
    a_cum_all = _dot_exact_lhs(jnp.where(causal, 1.0, 0.0).astype(BF16), a_all)
    a_cum_t = a_cum_all.T
    for g in range(SSD_GROUPS):
        _ssd_group(g, conv, a_cum_all, a_cum_t, dt_all, z_ref, dskip_ref, nw_ref, state_ref, o_ref, causal)


def _ssd(proj, lw, layer, batch, seq):
    t = proj.shape[0]
    nc = seq // CHUNK

    def rows(b, c):
        return pl.multiple_of((b * nc + c) * CHUNK, CHUNK)

    def window(width, offset):
        return pl.BlockSpec((pl.Element(CHUNK), pl.Element(width)), lambda b, c: (rows(b, c), offset))

    return pl.pallas_call(
        _ssd_body,
        grid=(batch, nc),
        in_specs=[
            window(SSD_CONV_DIM, OFF_XBC),
            window(SSD_INNER, OFF_Z),
            window(LANE, OFF_DT),
            pl.BlockSpec((None, SSD_CONV, SSD_CONV_DIM), lambda b, c: (layer, 0, 0)),
            pl.BlockSpec((None, 1, SSD_CONV_DIM), lambda b, c: (layer, 0, 0)),
            pl.BlockSpec((None, 1, LANE), lambda b, c: (layer, 0, 0)),
            pl.BlockSpec((None, 1, LANE), lambda b, c: (layer, 0, 0)),
            pl.BlockSpec((None, 1, SSD_INNER), lambda b, c: (layer, 0, 0)),
            pl.BlockSpec((None, 1, SSD_INNER), lambda b, c: (layer, 0, 0)),
        ],
        out_specs=pl.BlockSpec((CHUNK, SSD_INNER), lambda b, c: (b * nc + c, 0)),
        out_shape=jax.ShapeDtypeStruct((t, SSD_INNER), BF16),
        scratch_shapes=[
            pltpu.VMEM((SSD_N, SSD_INNER), F32),
            pltpu.VMEM((CHUNK, SSD_CONV_DIM), BF16),
            pltpu.VMEM(((SSD_CONV - 1) * CHUNK, 2 * CHUNK), BF16),
        ],
        compiler_params=_params(("parallel", "arbitrary")),
        name="ssd",
    )(proj, proj, proj, lw["conv_w"], lw["conv_b"], lw["dt_bias"], lw["a_log"], lw["d_skip"], lw["ssd_norm"])


def _ret_log_gamma(h):
    return float(np.log1p(-np.exp2(-5.0 - h)))


def _ret_body(q_ref, k_ref, v0_ref, v1_ref, g0_ref, g1_ref, nw_ref, o_ref, state_ref):
    lc = CHUNK
    half = RET_HEADS // 2
    v_refs = (v0_ref, v1_ref)
    g_refs = (g0_ref, g1_ref)

    @pl.when(pl.program_id(1) == 0)
    def _():
        state_ref[...] = jnp.zeros_like(state_ref)

    li = lax.broadcasted_iota(jnp.int32, (lc, lc), 0)
    si = lax.broadcasted_iota(jnp.int32, (lc, lc), 1)
    diff = (li - si).astype(F32)
    idx = lax.broadcasted_iota(jnp.int32, (lc, 1), 0).astype(F32)

    for h in range(RET_HEADS):
        log_g = _ret_log_gamma(h)
        dmat = jnp.where(diff >= 0, jnp.exp(diff * log_g), 0.0)
        k_dec = jnp.exp((lc - 1.0 - idx) * log_g)
        q_dec = jnp.exp((idx + 1.0) * log_g)
        q = q_ref[:, h * RET_DK:(h + 1) * RET_DK].astype(F32) * (RET_DK ** -0.5)
        k = k_ref[:, h * RET_DK:(h + 1) * RET_DK].astype(F32)
        vsl = slice((h % half) * RET_DV, (h % half + 1) * RET_DV)
        v = v_refs[h // half][:, vsl]
        scores = _dot_nt(q.astype(BF16), k.astype(BF16)) * dmat
        state = state_ref[h]
        y = _dot(scores.astype(BF16), v) + _dot((q * q_dec).astype(BF16), state.astype(BF16))
        chunk_kv = _dot((k * k_dec).T.astype(BF16), v)
        state_ref[h] = state * float(np.exp(lc * log_g)) + chunk_kv

        mu = jnp.mean(y, axis=-1, keepdims=True)
        yc = y - mu
        var = jnp.mean(yc * yc, axis=-1, keepdims=True)
        y = yc * lax.rsqrt(var + EPS) * nw_ref[h]
        gate = jax.nn.silu(g_refs[h // half][:, vsl].astype(F32))
        o_ref[:, h * RET_DV:(h + 1) * RET_DV] = (gate * y).astype(o_ref.dtype)


def _retention(proj, nw, layer, batch, seq):
    t = proj.shape[0]
    nc = seq // CHUNK
    hw = RET_V // 2
    q0 = OFF_RQK // RET_QK
    v0 = OFF_RV // hw
    g0 = OFF_RG // hw
    return pl.pallas_call(
        _ret_body,
        grid=(batch, nc),
        in_specs=[
            pl.BlockSpec((CHUNK, RET_QK), lambda b, c: (b * nc + c, q0)),
            pl.BlockSpec((CHUNK, RET_QK), lambda b, c: (b * nc + c, q0 + 1)),
            pl.BlockSpec((CHUNK, hw), lambda b, c: (b * nc + c, v0)),
            pl.BlockSpec((CHUNK, hw), lambda b, c: (b * nc + c, v0 + 1)),
            pl.BlockSpec((CHUNK, hw), lambda b, c: (b * nc + c, g0)),
            pl.BlockSpec((CHUNK, hw), lambda b, c: (b * nc + c, g0 + 1)),
            pl.BlockSpec((None, RET_HEADS, 1, RET_DV), lambda b, c: (layer, 0, 0, 0)),
        ],
        out_specs=pl.BlockSpec((CHUNK, RET_V), lambda b, c: (b * nc + c, 0)),
        out_shape=jax.ShapeDtypeStruct((t, RET_V), BF16),
        scratch_shapes=[pltpu.VMEM((RET_HEADS, RET_DK, RET_DV), F32)],
        compiler_params=_params(("parallel", "arbitrary")),
        name="retention",
    )(proj, proj, proj, proj, proj, proj, nw)


def _per_group_rows(seg, groups):
    depth, rows, d = seg.shape
    per = rows // groups
    seg = seg.reshape(depth, groups, per, d)
    seg = jnp.pad(seg, [(0, 0), (0, 0), (0, LANE - per), (0, 0)])
    return seg.reshape(depth, groups * LANE, d)


def _small_w_t(w_t):
    o_ng = IN_SPLITS[0] + IN_SPLITS[1]
    o_dt = o_ng + IN_SPLITS[2] + IN_SPLITS[3] + IN_SPLITS[4]
    ng = _per_group_rows(w_t[:, o_ng:o_ng + IN_SPLITS[2], :], NSA_GROUPS)
    dt = w_t[:, o_dt:o_dt + IN_SPLITS[5], :]
    pad = jnp.zeros((w_t.shape[0], N_SMALL - ng.shape[1] - dt.shape[1], w_t.shape[2]), w_t.dtype)
    return jnp.concatenate([ng, dt, pad], axis=1)


def _pad_lanes(v):
    depth, heads = v.shape
    return jnp.pad(v, [(0, 0), (0, LANE - heads)]).reshape(depth, 1, LANE)


def _layer(x, lw, layer, batch, seq):
    u = _rms(x, lw["norm_mix"], layer, BF16)
    proj, xc = _inproj(u, lw["w_t"], lw["w_small_t"], layer)
    cmp = _compress(xc, lw["cmp_pe"][layer], lw["cmp_w1"][layer], lw["cmp_w2"][layer], batch)

    y_nsa = _nsa(proj, cmp, batch, seq)
    y_ssd = _ssd(proj, lw, layer, batch, seq)
    y_ret = _retention(proj, lw["ret_norm"], layer, batch, seq)

    merged = _merge(y_nsa, y_ssd, y_ret, proj, lw["p_nsa"], lw["p_ssd"], lw["p_ret"], layer)
    x = _mm_res(merged, lw["w_out"], x, layer, 512)
    f = _rms(x, lw["norm_ffn"], layer, BF16)
    hidden = _ffn_up(f, lw["w_gate"], lw["w_up"], layer)
    return _mm_res(hidden, lw["w_down"], x, layer, 256)


def kernel(x, norm_mix, w_in, cmp_k_pe, cmp_k_w1, cmp_k_w2, cmp_v_pe, cmp_v_w1, cmp_v_w2, conv_w, conv_b, dt_bias, a_log, d_skip, ssd_norm, ret_norm, p_nsa, p_ssd, p_ret, w_out, norm_ffn, w_gate, w_up, w_down, norm_final):
    batch, seq, d = x.shape
    depth = w_in.shape[0]
    w_t = jnp.swapaxes(w_in, 1, 2)
    pe = jnp.stack([cmp_k_pe, cmp_v_pe], axis=1).reshape(depth, 2, 1, CMP_LEN * DH)
    lw = {
        "norm_mix": norm_mix[:, None, :],
        "w_t": w_t,
        "w_small_t": _small_w_t(w_t),
        "cmp_pe": jnp.pad(pe, [(0, 0), (0, 0), (0, 7), (0, 0)]).astype(BF16),
        "cmp_w1": jnp.stack([cmp_k_w1, cmp_v_w1], axis=1).astype(BF16),
        "cmp_w2": jnp.stack([cmp_k_w2, cmp_v_w2], axis=1).astype(BF16),
        "conv_w": conv_w,
        "conv_b": conv_b[:, None, :],
        "dt_bias": _pad_lanes(dt_bias),
        "a_log": _pad_lanes(a_log),
        "d_skip": jnp.repeat(d_skip, SSD_P, axis=-1)[:, None, :],
        "ssd_norm": ssd_norm[:, None, :],
        "ret_norm": ret_norm[:, :, None, :],
        "p_nsa": p_nsa, "p_ssd": p_ssd, "p_ret": p_ret, "w_out": w_out,
        "norm_ffn": norm_ffn[:, None, :],
        "w_gate": w_gate, "w_up": w_up, "w_down": w_down,
    }
    h = x.reshape(batch * seq, d)
    for layer in range(depth):
        h = _layer(h, lw, layer, batch, seq)
    return _rms(h, norm_final[None, None, :], 0, F32).reshape(batch, seq, d)
```
